```python
import math
import jax
import jax.numpy as jnp
from jax import lax
import numpy as np

D_MODEL = 2048
BATCH = 4
SEQ = 2048
DEPTH = 4
DEC_BATCH = 2
DEC_SEQ = 8192
PAST_LEN = 128

HEAD_DIM = 128
ROPE_THETA = 10000.0
N_EVEN = (DEPTH + 1) // 2
N_ODD = DEPTH // 2

A_HEADS = D_MODEL // (2 * HEAD_DIM)
A_KV_HEADS = A_HEADS // 4
A_WINDOW = 128
A_BLOCK = 128
B_HEADS = D_MODEL // (2 * HEAD_DIM)
B_DIM = HEAD_DIM // 2
B_QBLOCK = 128
C_CHANNELS = D_MODEL // 2
C_WIDTH = 3
D_PATTERNS = ((128, 1), (512, 4), (2048, 16))
D_GROUPS = len(D_PATTERNS)
D_HEADS_PER_GROUP = 4
D_QBLOCK = 128
D_REACH = max(w // 2 for w, _ in D_PATTERNS)
D_NK = max(w // (2 * d) for w, d in D_PATTERNS)
MEM_TOKENS = 256
MEM_HEADS = 4
MEM_HEAD_DIM = 128
MEM_INNER = MEM_HEADS * MEM_HEAD_DIM
N_EXPERTS = 32
TOP_K = 4
D_FF_EXPERT = D_MODEL
SWIGLU_LIMIT = 7.0
SWIGLU_ALPHA = 1.702
MOE_BLOCK = 256
LN_EPS = 1e-5
DEEPNORM_ALPHA = (2 * DEPTH) ** 0.25
DEEPNORM_BETA = (8 * DEPTH) ** -0.25
NEG_INF = -1e30

EVEN_SIZES = (A_HEADS * HEAD_DIM, A_KV_HEADS * HEAD_DIM, A_KV_HEADS * HEAD_DIM,
              B_HEADS * 2 * B_DIM, B_HEADS * 2 * B_DIM, B_HEADS * 2 * B_DIM)
EVEN_IN = sum(EVEN_SIZES)
EVEN_OUT = A_HEADS * HEAD_DIM + B_HEADS * 2 * B_DIM
D_QKV = D_GROUPS * D_HEADS_PER_GROUP * HEAD_DIM
ODD_SIZES = (C_CHANNELS, C_CHANNELS, C_CHANNELS, D_QKV, D_QKV, D_QKV)
ODD_IN = sum(ODD_SIZES)
ODD_OUT = C_CHANNELS + D_HEADS_PER_GROUP * HEAD_DIM

kernel_name = 'hybrid_bidir_encoder_swa_diff_conv_dilated_moe'


def split_cols(h, sizes):
    return jnp.split(h, np.cumsum(sizes)[:-1].tolist(), axis=-1)


def rope_tables(s, dim):
    inv = ROPE_THETA ** (-jnp.arange(0, dim, 2, dtype=jnp.float32) / dim)
    ang = jnp.arange(s, dtype=jnp.float32)[:, None] * inv[None, :]
    return jnp.cos(ang), jnp.sin(ang)


def apply_rope(x, cos, sin):
    shape = (1, cos.shape[0]) + (1,) * (x.ndim - 3) + (cos.shape[1],)
    c = cos.reshape(shape)
    s = sin.reshape(shape)
    x1, x2 = jnp.split(x.astype(jnp.float32), 2, axis=-1)
    return jnp.concatenate([x1 * c - x2 * s, x2 * c + x1 * s], axis=-1).astype(x.dtype)


def layer_norm(x, g, b):
    xf = x.astype(jnp.float32)
    xc = xf - jnp.mean(xf, -1, keepdims=True)
    var = jnp.mean(xc * xc, -1, keepdims=True)
    y = xc * lax.rsqrt(var + LN_EPS) * g.astype(jnp.float32) + b.astype(jnp.float32)
    return y.astype(x.dtype)


def window_gqa_sink(q, k, v, sink):
    bsz, s, hq, hd = q.shape
    kvh = k.shape[2]
    g = hq // kvh
    nb = s // A_BLOCK
    qb = q.reshape(bsz, nb, A_BLOCK, kvh, g, hd)

    def band(t):
        tp = jnp.pad(t, ((0, 0), (A_BLOCK, A_BLOCK), (0, 0), (0, 0))).reshape(bsz, nb + 2, A_BLOCK, kvh, hd)
        return jnp.concatenate([tp[:, :-2], tp[:, 1:-1], tp[:, 2:]], axis=2)

    kw, vw = band(k), band(v)
    sc = jnp.einsum('bnqkgd,bnjkd->bnkgqj', qb, kw).astype(jnp.float32) * hd ** -0.5
    blk = jnp.arange(nb)[:, None, None] * A_BLOCK
    qpos = blk + jnp.arange(A_BLOCK)[None, :, None]
    kpos = blk - A_BLOCK + jnp.arange(3 * A_BLOCK)[None, None, :]
    valid = (jnp.abs(qpos - kpos) <= A_WINDOW) & (kpos >= 0) & (kpos < s)
    sc = jnp.where(valid[None, :, None, None], sc, NEG_INF)
    sk = sink.astype(jnp.float32).reshape(1, 1, kvh, g, 1, 1)
    m = jnp.maximum(jnp.max(sc, -1, keepdims=True), sk)
    p = jnp.exp(sc - m)
    p = p / (jnp.sum(p, -1, keepdims=True) + jnp.exp(sk - m))
    o = jnp.einsum('bnkgqj,bnjkd->bnqkgd', p.astype(v.dtype), vw)
    return o.reshape(bsz, s, hq, hd)


def diff_attention(q, k, v, lam, norm_w, lambda_init):
    bsz, s, h, _, dd = q.shape
    nb = s // B_QBLOCK
    qb = q.reshape(bsz, nb, B_QBLOCK, h, 2, dd).swapaxes(0, 1)

    def one_block(qblk):
        sc = jnp.einsum('bqhcd,bkhcd->bhcqk', qblk, k).astype(jnp.float32) * dd ** -0.5
        p = jax.nn.softmax(sc, axis=-1)
        a = p[:, :, 0] - lam * p[:, :, 1]
        return jnp.einsum('bhqk,bkhe->bqhe', a.astype(v.dtype), v)

    o = lax.map(one_block, qb).swapaxes(0, 1).reshape(bsz, s, h, 2 * dd)
    of = o.astype(jnp.float32)
    of = of * lax.rsqrt(jnp.mean(of * of, -1, keepdims=True) + LN_EPS) * norm_w.astype(jnp.float32)
    return (of * (1.0 - lambda_init)).astype(v.dtype)


def dilated_attention(q, k, v):
    bsz, s, ng, hg, hd = q.shape
    nb = s // D_QBLOCK
    steps = jnp.arange(-D_NK, D_NK + 1)
    dil = jnp.array([d for _, d in D_PATTERNS], dtype=jnp.int32)
    reach = jnp.array([w // (2 * d) for w, d in D_PATTERNS], dtype=jnp.int32)
    offs = dil[:, None] * steps[None, :]
    step_ok = jnp.abs(steps)[None, :] <= reach[:, None]
    pad = ((0, 0), (D_REACH, D_REACH), (0, 0), (0, 0), (0, 0))
    kp, vp = jnp.pad(k, pad), jnp.pad(v, pad)
    gidx = jnp.arange(ng)[:, None, None]
    qb = q.reshape(bsz, nb, D_QBLOCK, ng, hg, hd).swapaxes(0, 1)

    def one_block(args):
        qblk, i = args
        qpos = i * D_QBLOCK + jnp.arange(D_QBLOCK)
        kpos = qpos[None, :, None] + offs[:, None, :]
        valid = (kpos >= 0) & (kpos < s) & step_ok[:, None, :]
        kg = kp[:, kpos + D_REACH, gidx]
        vg = vp[:, kpos + D_REACH, gidx]
        sc = jnp.einsum('bqghd,bgqjhd->bghqj', qblk, kg).astype(jnp.float32) * hd ** -0.5
        sc = jnp.where(valid[None, :, None], sc, NEG_INF)
        m = jnp.max(sc, -1, keepdims=True)
        p = jnp.exp(sc - m)
        l = jnp.sum(p, -1, keepdims=True)
        o = jnp.einsum('bghqj,bgqjhd->bghqd', (p / l).astype(v.dtype), vg).astype(jnp.float32)
        wgt = jax.nn.softmax((m + jnp.log(l))[..., 0], axis=1)
        return jnp.einsum('bghq,bghqd->bqhd', wgt, o).astype(v.dtype)

    o = lax.map(one_block, (qb, jnp.arange(nb)))
    return o.swapaxes(0, 1).reshape(bsz, s, hg, hd)


def even_mixer(x, w_in, w_out, sink, lq1, lk1, lq2, lk2, dnorm_w, lambda_init, rope_h, rope_b):
    bsz, s, _ = x.shape
    aq, ak, av, bq, bk, bv = split_cols(x @ w_in, EVEN_SIZES)
    aq = apply_rope(aq.reshape(bsz, s, A_HEADS, HEAD_DIM), *rope_h)
    ak = apply_rope(ak.reshape(bsz, s, A_KV_HEADS, HEAD_DIM), *rope_h)
    av = av.reshape(bsz, s, A_KV_HEADS, HEAD_DIM)
    oa = window_gqa_sink(aq, ak, av, sink).reshape(bsz, s, A_HEADS * HEAD_DIM)
    bq = apply_rope(bq.reshape(bsz, s, B_HEADS, 2, B_DIM), *rope_b)
    bk = apply_rope(bk.reshape(bsz, s, B_HEADS, 2, B_DIM), *rope_b)
    bv = bv.reshape(bsz, s, B_HEADS, 2 * B_DIM)
    f32 = jnp.float32
    lam = (jnp.exp(jnp.sum(lq1.astype(f32) * lk1.astype(f32)))
           - jnp.exp(jnp.sum(lq2.astype(f32) * lk2.astype(f32))) + lambda_init)
    ob = diff_attention(bq, bk, bv, lam, dnorm_w, lambda_init).reshape(bsz, s, B_HEADS * 2 * B_DIM)
    return jnp.concatenate([oa, ob], axis=-1) @ w_out


def odd_mixer(x, w_in, w_out, conv_w, rope_h):
    bsz, s, _ = x.shape
    cb, cc, cx, dq, dk, dv = split_cols(x @ w_in, ODD_SIZES)
    u = cc * cx
    half = C_WIDTH // 2
    up = jnp.pad(u, ((0, 0), (half, half), (0, 0)))
    conv = sum(up[:, t:t + s] * conv_w[t] for t in range(C_WIDTH))
    oc = cb * conv
    shp = (bsz, s, D_GROUPS, D_HEADS_PER_GROUP, HEAD_DIM)
    dq = apply_rope(dq.reshape(shp), *rope_h)
    dk = apply_rope(dk.reshape(shp), *rope_h)
    od = dilated_attention(dq, dk, dv.reshape(shp)).reshape(bsz, s, D_HEADS_PER_GROUP * HEAD_DIM)
    return jnp.concatenate([oc, od], axis=-1) @ w_out


def memory_attention(x, mem, wq, wkv, wo):
    bsz, s, _ = x.shape
    q = (x @ wq).reshape(bsz, s, MEM_HEADS, MEM_HEAD_DIM)
    k, v = split_cols(mem @ wkv, (MEM_INNER, MEM_INNER))
    k = k.reshape(bsz, -1, MEM_HEADS, MEM_HEAD_DIM)
    v = v.reshape(bsz, -1, MEM_HEADS, MEM_HEAD_DIM)
    sc = jnp.einsum('bshd,bmhd->bhsm', q, k).astype(jnp.float32) * MEM_HEAD_DIM ** -0.5
    p = jax.nn.softmax(sc, axis=-1).astype(v.dtype)
    o = jnp.einsum('bhsm,bmhd->bshd', p, v).reshape(bsz, s, MEM_INNER)
    return o @ wo


def moe(x2d, l, router_w, router_b, w_gu, b_gu, w_down, b_down):
    t = x2d.shape[0]
    m_rows = t * TOP_K
    logits = (x2d @ router_w[l] + router_b[l]).astype(jnp.float32)
    top_v, top_i = lax.top_k(logits, TOP_K)
    gates = jax.nn.softmax(top_v, axis=-1)
    flat_e = top_i.reshape(-1).astype(jnp.int32)
    order = jnp.argsort(flat_e)
    sorted_e = flat_e[order]
    counts = jnp.bincount(flat_e, length=N_EXPERTS)
    starts = jnp.cumsum(counts) - counts
    pcounts = (counts + MOE_BLOCK - 1) // MOE_BLOCK * MOE_BLOCK
    pends = jnp.cumsum(pcounts)
    pstarts = pends - pcounts
    dest = pstarts[sorted_e] + jnp.arange(m_rows) - starts[sorted_e]
    n_blocks = -(-(m_rows + N_EXPERTS * (MOE_BLOCK - 1)) // MOE_BLOCK)
    p_rows = n_blocks * MOE_BLOCK
    row_tok = jnp.full((p_rows,), t, jnp.int32).at[dest].set((order // TOP_K).astype(jnp.int32))
    row_gate = jnp.zeros((p_rows,), jnp.float32).at[dest].set(gates.reshape(-1)[order])
    blk_e = jnp.minimum(jnp.searchsorted(pends, jnp.arange(n_blocks) * MOE_BLOCK, side='right'),
                        N_EXPERTS - 1).astype(jnp.int32)
    xpad = jnp.concatenate([x2d, jnp.zeros((1, x2d.shape[1]), x2d.dtype)], axis=0)

    def expert_block(args):
        rows, e = args
        h = xpad[rows] @ w_gu[l, e] + b_gu[l, e]
        gate, up = jnp.split(h, 2, axis=-1)
        gate = jnp.minimum(gate, SWIGLU_LIMIT)
        up = jnp.clip(up, -SWIGLU_LIMIT, SWIGLU_LIMIT)
        a = gate * jax.nn.sigmoid(SWIGLU_ALPHA * gate) * (up + 1.0)
        return a @ w_down[l, e] + b_down[l, e]

    ys = lax.map(expert_block, (row_tok.reshape(n_blocks, MOE_BLOCK), blk_e)).reshape(p_rows, -1)
    out = jnp.zeros((t + 1, x2d.shape[1]), x2d.dtype).at[row_tok].add(ys * row_gate[:, None].astype(ys.dtype))
    return out[:t]


def trunk(x, mem, weights):
    (ev_w_in, ev_w_out, ev_sink, ev_lam_q1, ev_lam_k1, ev_lam_q2, ev_lam_k2, ev_dnorm_w,
     od_w_in, od_w_out, od_conv_w, mem_wq, mem_wkv, mem_wo, ln_g, ln_b,
     moe_router_w, moe_router_b, moe_w_gu, moe_b_gu, moe_w_down, moe_b_down) = weights
    bsz, s, d = x.shape
    rope_h = rope_tables(s, HEAD_DIM)
    rope_b = rope_tables(s, B_DIM)
    for l in range(DEPTH):
        j = l // 2
        if l % 2 == 0:
            lambda_init = 0.8 - 0.6 * math.exp(-0.3 * l)
            f = even_mixer(x, ev_w_in[j], ev_w_out[j], ev_sink[j], ev_lam_q1[j], ev_lam_k1[j],
                           ev_lam_q2[j], ev_lam_k2[j], ev_dnorm_w[j], lambda_init, rope_h, rope_b)
        else:
            f = odd_mixer(x, od_w_in[j], od_w_out[j], od_conv_w[j], rope_h)
        x = layer_norm(DEEPNORM_ALPHA * x + f, ln_g[l, 0], ln_b[l, 0])
        f = memory_attention(x, mem, mem_wq[l], mem_wkv[l], mem_wo[l])
        x = layer_norm(DEEPNORM_ALPHA * x + f, ln_g[l, 1], ln_b[l, 1])
        f = moe(x.reshape(bsz * s, d), l, moe_router_w, moe_router_b, moe_w_gu, moe_b_gu,
                moe_w_down, moe_b_down).reshape(bsz, s, d)
        x = layer_norm(DEEPNORM_ALPHA * x + f, ln_g[l, 2], ln_b[l, 2])
    return x


def setup_inputs(seed: int = 0) -> dict:
    key = jax.random.key(seed)
    ks = jax.random.split(key, 32)

    def nrm(i, shape, scale):
        return jax.random.normal(ks[i], shape, jnp.float32) * scale

    beta = DEEPNORM_BETA
    return {
        'x_prompt': nrm(0, (BATCH, SEQ, D_MODEL), 1.0),
        'x_sample': nrm(1, (DEC_BATCH, DEC_SEQ, D_MODEL), 1.0),
        'mem_prompt': nrm(2, (BATCH, MEM_TOKENS, D_MODEL), 1.0),
        'mem_sample': nrm(3, (DEC_BATCH, MEM_TOKENS, D_MODEL), 1.0),
        'ev_w_in': nrm(4, (N_EVEN, D_MODEL, EVEN_IN), D_MODEL ** -0.5),
        'ev_w_out': nrm(5, (N_EVEN, EVEN_OUT, D_MODEL), beta * EVEN_OUT ** -0.5),
        'ev_sink': nrm(6, (N_EVEN, A_HEADS), 0.5),
        'ev_lam_q1': nrm(7, (N_EVEN, B_DIM), 0.1),
        'ev_lam_k1': nrm(8, (N_EVEN, B_DIM), 0.1),
        'ev_lam_q2': nrm(9, (N_EVEN, B_DIM), 0.1),
        'ev_lam_k2': nrm(10, (N_EVEN, B_DIM), 0.1),
        'ev_dnorm_w': 1.0 + nrm(11, (N_EVEN, 2 * B_DIM), 0.02),
        'od_w_in': nrm(12, (N_ODD, D_MODEL, ODD_IN), D_MODEL ** -0.5),
        'od_w_out': nrm(13, (N_ODD, ODD_OUT, D_MODEL), beta * ODD_OUT ** -0.5),
        'od_conv_w': nrm(14, (N_ODD, C_WIDTH, C_CHANNELS), C_WIDTH ** -0.5),
        'mem_wq': nrm(15, (DEPTH, D_MODEL, MEM_INNER), D_MODEL ** -0.5),
        'mem_wkv': nrm(16, (DEPTH, D_MODEL, 2 * MEM_INNER), D_MODEL ** -0.5),
        'mem_wo': nrm(17, (DEPTH, MEM_INNER, D_MODEL), beta * MEM_INNER ** -0.5),
        'ln_g': 1.0 + nrm(18, (DEPTH, 3, D_MODEL), 0.02),
        'ln_b': nrm(19, (DEPTH, 3, D_MODEL), 0.02),
        'moe_router_w': nrm(20, (DEPTH, D_MODEL, N_EXPERTS), D_MODEL ** -0.5),
        'moe_router_b': nrm(21, (DEPTH, N_EXPERTS), 0.01),
        'moe_w_gu': nrm(22, (DEPTH, N_EXPERTS, D_MODEL, 2 * D_FF_EXPERT), D_MODEL ** -0.5),
        'moe_b_gu': nrm(23, (DEPTH, N_EXPERTS, 2 * D_FF_EXPERT), 0.01),
        'moe_w_down': nrm(24, (DEPTH, N_EXPERTS, D_FF_EXPERT, D_MODEL), beta * D_FF_EXPERT ** -0.5),
        'moe_b_down': nrm(25, (DEPTH, N_EXPERTS, D_MODEL), 0.01),
    }


def reference(x_prompt, x_sample, mem_prompt, mem_sample, ev_w_in, ev_w_out, ev_sink, ev_lam_q1,
              ev_lam_k1, ev_lam_q2, ev_lam_k2, ev_dnorm_w, od_w_in, od_w_out, od_conv_w, mem_wq,
              mem_wkv, mem_wo, ln_g, ln_b, moe_router_w, moe_router_b, moe_w_gu, moe_b_gu,
              moe_w_down, moe_b_down):
    weights = (ev_w_in, ev_w_out, ev_sink, ev_lam_q1, ev_lam_k1, ev_lam_q2, ev_lam_k2, ev_dnorm_w,
               od_w_in, od_w_out, od_conv_w, mem_wq, mem_wkv, mem_wo, ln_g, ln_b,
               moe_router_w, moe_router_b, moe_w_gu, moe_b_gu, moe_w_down, moe_b_down)
    y_prompt = trunk(x_prompt, mem_prompt, weights)
    y_sample = trunk(x_sample, mem_sample, weights)
    return (y_prompt, y_sample)
```

```python
import functools
import math

import jax
import jax.numpy as jnp
from jax import lax
from jax.experimental import pallas as pl
from jax.experimental.pallas import tpu as pltpu

F32 = jnp.float32
BF16 = jnp.bfloat16
LANES = 128
VMEM_LIMIT = 56 * 1024 * 1024

HEAD_DIM = 128
ROPE_THETA = 10000.0
A_WINDOW = 128
B_DIM = 64
C_WIDTH = 3
D_PATTERNS = ((128, 1), (512, 4), (2048, 16))
D_HEADS_PER_GROUP = 4
D_NK = 64
MEM_HEADS = 4
TOP_K = 4
SWIGLU_LIMIT = 7.0
SWIGLU_ALPHA = 1.702
LN_EPS = 1e-5
NEG_INF = -1e30

MODE_NONE, MODE_ROPE128, MODE_ROPE128_Q, MODE_ROPE64, MODE_ROPE64_Q = 0, 1, 2, 3, 4


def _cparams(sem):
    return pltpu.CompilerParams(dimension_semantics=sem, vmem_limit_bytes=VMEM_LIMIT)


def _dot(a, b):
    return jnp.dot(a, b, preferred_element_type=F32)


def _dot_t(a, b):
    return lax.dot_general(a, b, (((1,), (1,)), ((), ())), preferred_element_type=F32)


class Segs:
    def __init__(self, b1, s1, b2, s2):
        self.b1, self.s1, self.b2, self.s2 = b1, s1, b2, s2
        self.t1 = b1 * s1
        self.t = b1 * s1 + b2 * s2

    def groups(self):
        return ((0, self.b1, self.s1, 0), (self.t1, self.b2, self.s2, self.b1))

    def pos_tile(self, i, tm):
        n1 = self.t1 // tm
        return jnp.where(i < n1, i % (self.s1 // tm), (i - n1) % (self.s2 // tm))

    def batch_of_tile(self, i, tm):
        n1 = self.t1 // tm
        return jnp.where(i < n1, (i * tm) // self.s1, self.b1 + ((i - n1) * tm) // self.s2)


def _layer_norm(z, g, b):
    mu = jnp.mean(z, axis=-1, keepdims=True)
    zc = z - mu
    var = jnp.mean(zc * zc, axis=-1, keepdims=True)
    return zc * lax.rsqrt(var + LN_EPS) * g + b


def _proj_kernel(mode_ref, x_ref, w_ref, c128_ref, s128_ref, c64_ref, sa64_ref, sb64_ref, o_ref,
                 *, n_slabs, slab0, q128_scale, q64_scale):
    j = pl.program_id(0)
    acc = _dot(x_ref[...], w_ref[...])
    for s in range(n_slabs):
        a = acc[:, s * LANES:(s + 1) * LANES]
        mode = mode_ref[slab0 + j * n_slabs + s]
        cols = slice(s * LANES, (s + 1) * LANES)

        @pl.when(mode == MODE_NONE)
        def _():
            o_ref[:, cols] = a.astype(o_ref.dtype)

        def rope128(scale):
            r = a * c128_ref[...] + pltpu.roll(a, 64, 1) * s128_ref[...]
            o_ref[:, cols] = (r * scale).astype(o_ref.dtype)

        def rope64(scale):
            r = (a * c64_ref[...] + pltpu.roll(a, 96, 1) * sa64_ref[...]
                 + pltpu.roll(a, 32, 1) * sb64_ref[...])
            o_ref[:, cols] = (r * scale).astype(o_ref.dtype)

        pl.when(mode == MODE_ROPE128)(functools.partial(rope128, 1.0))
        pl.when(mode == MODE_ROPE128_Q)(functools.partial(rope128, q128_scale))
        pl.when(mode == MODE_ROPE64)(functools.partial(rope64, 1.0))
        pl.when(mode == MODE_ROPE64_Q)(functools.partial(rope64, q64_scale))


def proj(xb, w, modes, tables, segs, *, col0, ncols, tm, tn, out_dtype):
    m, k = xb.shape
    assert ncols % tn == 0 and col0 % tn == 0 and m % tm == 0 and tn % LANES == 0
    n_slabs = tn // LANES
    jb = col0 // tn
    tab_spec = pl.BlockSpec((tm, LANES), lambda j, i, *_: (segs.pos_tile(i, tm), 0))
    kern = functools.partial(_proj_kernel, n_slabs=n_slabs, slab0=col0 // LANES,
                             q128_scale=HEAD_DIM ** -0.5, q64_scale=B_DIM ** -0.5)
    return pl.pallas_call(
        kern,
        grid_spec=pltpu.PrefetchScalarGridSpec(
            num_scalar_prefetch=1,
            grid=(ncols // tn, m // tm),
            in_specs=[pl.BlockSpec((tm, k), lambda j, i, *_: (i, 0)),
                      pl.BlockSpec((k, tn), lambda j, i, *_: (0, jb + j)),
                      tab_spec, tab_spec, tab_spec, tab_spec, tab_spec],
            out_specs=pl.BlockSpec((tm, tn), lambda j, i, *_: (i, j)),
        ),
        out_shape=jax.ShapeDtypeStruct((m, ncols), out_dtype),
        compiler_params=_cparams(("arbitrary", "arbitrary")),
        name="proj",
    )(modes, xb, w, *tables)


def rope_tables(s):
    inv = ROPE_THETA ** (-jnp.arange(0, HEAD_DIM, 2, dtype=F32) / HEAD_DIM)
    ang = jnp.arange(s, dtype=F32)[:, None] * inv[None, :]
    c, sn = jnp.cos(ang), jnp.sin(ang)
    c128 = jnp.concatenate([c, c], axis=1)
    s128 = jnp.concatenate([-sn, sn], axis=1)
    invb = ROPE_THETA ** (-jnp.arange(0, B_DIM, 2, dtype=F32) / B_DIM)
    angb = jnp.arange(s, dtype=F32)[:, None] * invb[None, :]
    cb, sb = jnp.cos(angb), jnp.sin(angb)
    z = jnp.zeros_like(sb)
    c64 = jnp.concatenate([cb, cb, cb, cb], axis=1)
    sa64 = jnp.concatenate([-sb, z, -sb, z], axis=1)
    sb64 = jnp.concatenate([z, sb, z, sb], axis=1)
    return (c128, s128, c64, sa64, sb64)


def _window_kernel(sink_ref, q_ref, kp_ref, kc_ref, kn_ref, vp_ref, vc_ref, vn_ref, o_ref,
                   *, n_heads, n_kv, blk):
    n = pl.program_id(1)
    nb = pl.num_programs(1)
    r = lax.broadcasted_iota(jnp.int32, (blk, 3 * blk), 0)
    c = lax.broadcasted_iota(jnp.int32, (blk, 3 * blk), 1)
    valid = jnp.abs(r + blk - c) <= A_WINDOW
    valid &= (c >= blk) | (n > 0)
    valid &= (c < 2 * blk) | (n < nb - 1)
    g = n_heads // n_kv
    for kv in range(n_kv):
        cs = slice(kv * HEAD_DIM, (kv + 1) * HEAD_DIM)
        k3 = jnp.concatenate([kp_ref[:, cs], kc_ref[:, cs], kn_ref[:, cs]], axis=0)
        v3 = jnp.concatenate([vp_ref[:, cs], vc_ref[:, cs], vn_ref[:, cs]], axis=0)
        for gi in range(g):
            h = kv * g + gi
            hs = slice(h * HEAD_DIM, (h + 1) * HEAD_DIM)
            sc = jnp.where(valid, _dot_t(q_ref[:, hs], k3), NEG_INF)
            sk = sink_ref[h]
            m = jnp.maximum(jnp.max(sc, axis=-1, keepdims=True), sk)
            p = jnp.exp(sc - m)
            denom = jnp.sum(p, axis=-1, keepdims=True) + jnp.exp(sk - m)
            o = _dot(p.astype(BF16), v3) / denom
            o_ref[:, hs] = o.astype(o_ref.dtype)


def window_attention(h, sink, row0, bsz, s, *, d_model):
    blk = A_WINDOW
    n_heads = d_model // (2 * HEAD_DIM)
    n_kv = n_heads // 4
    qw, kw = n_heads * HEAD_DIM, n_kv * HEAD_DIM
    nb = s // blk
    rb0 = row0 // blk
    kcol, vcol = qw // kw, qw // kw + 1

    def rows(b, n):
        return rb0 + b * nb + n

    def spec(col, shift):
        def imap(b, n):
            nn = jnp.clip(n + shift, 0, nb - 1)
            return (rows(b, nn), col)
        return pl.BlockSpec((blk, kw), imap)

    kern = functools.partial(_window_kernel, n_heads=n_heads, n_kv=n_kv, blk=blk)
    return pl.pallas_call(
        kern,
        grid=(bsz, nb),
        in_specs=[pl.BlockSpec(memory_space=pltpu.SMEM),
                  pl.BlockSpec((blk, qw), lambda b, n: (rows(b, n), 0)),
                  spec(kcol, -1), spec(kcol, 0), spec(kcol, 1),
                  spec(vcol, -1), spec(vcol, 0), spec(vcol, 1)],
        out_specs=pl.BlockSpec((blk, qw), lambda b, n: (b * nb + n, 0)),
        out_shape=jax.ShapeDtypeStruct((bsz * s, qw), BF16),
        compiler_params=_cparams(("arbitrary", "arbitrary")),
        name="window_attn",
    )(sink, h, h, h, h, h, h, h)


def _diff_kernel(lamv_ref, nw_ref, q_ref, k_ref, v_ref, o_ref, *, tq, tk, lambda_init):
    s = k_ref.shape[0]
    lane = lax.broadcasted_iota(jnp.int32, (tq, LANES), 1)
    q = q_ref[...]
    zero = jnp.zeros_like(q)
    q2 = jnp.concatenate([jnp.where(lane < B_DIM, q, zero), jnp.where(lane >= B_DIM, q, zero)],
                         axis=0)

    def body(i, carry):
        m, l, acc = carry
        ks = pl.multiple_of(i * tk, tk)
        kb = k_ref[pl.ds(ks, tk), :]
        vb = v_ref[pl.ds(ks, tk), :]
        sc = _dot_t(q2, kb)
        m_new = jnp.maximum(m, jnp.max(sc, axis=-1, keepdims=True))
        a = jnp.exp(m - m_new)
        p = jnp.exp(sc - m_new)
        l = a * l + jnp.sum(p, axis=-1, keepdims=True)
        acc = a * acc + _dot(p.astype(BF16), vb)
        return m_new, l, acc

    m0 = jnp.full((2 * tq, 1), NEG_INF, F32)
    l0 = jnp.zeros((2 * tq, 1), F32)
    a0 = jnp.zeros((2 * tq, LANES), F32)
    m, l, acc = lax.fori_loop(0, s // tk, body, (m0, l0, a0))
    on = acc / l
    lv = lamv_ref[...]
    lam = (jnp.exp(jnp.sum(lv[0:1] * lv[1:2], axis=-1, keepdims=True))
           - jnp.exp(jnp.sum(lv[2:3] * lv[3:4], axis=-1, keepdims=True)) + lambda_init)
    of = on[:tq] - lam * on[tq:]
    of = of * lax.rsqrt(jnp.mean(of * of, axis=-1, keepdims=True) + LN_EPS) * nw_ref[...]
    o_ref[...] = (of * (1.0 - lambda_init)).astype(o_ref.dtype)


def diff_attention(h, lamv, norm_w, row0, bsz, s, lambda_init, *, d_model, tq, tk):
    n_heads = d_model // (2 * HEAD_DIM)
    a_cols = (n_heads + 2 * (n_heads // 4))
    qc, kc, vc = a_cols, a_cols + n_heads, a_cols + 2 * n_heads
    nq = s // tq
    sb0 = row0 // s
    qb0 = row0 // tq
    kern = functools.partial(_diff_kernel, tq=tq, tk=tk, lambda_init=lambda_init)
    return pl.pallas_call(
        kern,
        grid=(bsz, n_heads, nq),
        in_specs=[pl.BlockSpec((4, B_DIM), lambda b, hh, n: (0, 0)),
                  pl.BlockSpec((1, LANES), lambda b, hh, n: (0, 0)),
                  pl.BlockSpec((tq, LANES), lambda b, hh, n: (qb0 + b * nq + n, qc + hh)),
                  pl.BlockSpec((s, LANES), lambda b, hh, n: (sb0 + b, kc + hh)),
                  pl.BlockSpec((s, LANES), lambda b, hh, n: (sb0 + b, vc + hh))],
        out_specs=pl.BlockSpec((tq, LANES), lambda b, hh, n: (b * nq + n, hh)),
        out_shape=jax.ShapeDtypeStruct((bsz * s, n_heads * LANES), BF16),
        compiler_params=_cparams(("arbitrary", "arbitrary", "arbitrary")),
        name="diff_attn",
    )(lamv, norm_w, h, h, h)


def _conv_kernel(cw_ref, b_ref, c_ref, x_ref, cp_ref, xp_ref, cn_ref, xn_ref, o_ref, *, ts):
    n = pl.program_id(1)
    nt = pl.num_programs(1)
    u = c_ref[...] * x_ref[...]
    prev = jnp.where(n > 0, cp_ref[7:8, :] * xp_ref[7:8, :], 0.0)
    nxt = jnp.where(n < nt - 1, cn_ref[0:1, :] * xn_ref[0:1, :], 0.0)
    row = lax.broadcasted_iota(jnp.int32, u.shape, 0)
    um1 = jnp.where(row == 0, prev, pltpu.roll(u, 1, 0))
    up1 = jnp.where(row == ts - 1, nxt, pltpu.roll(u, ts - 1, 0))
    conv = um1 * cw_ref[0:1, :] + u * cw_ref[1:2, :] + up1 * cw_ref[2:3, :]
    o_ref[...] = (b_ref[...] * conv).astype(o_ref.dtype)


def gated_conv(hc, conv_w, row0, bsz, s, *, ts, tc):
    c = hc.shape[1] // 3
    nt = s // ts
    nc = c // tc
    rb0 = row0 // ts
    r8 = ts // 8

    def main(col):
        return pl.BlockSpec((ts, tc), lambda b, n, j: (rb0 + b * nt + n, col * nc + j))

    def halo(col, shift):
        def imap(b, n, j):
            blk8 = (rb0 + b * nt + n) * r8 + (-1 if shift < 0 else r8)
            return (jnp.clip(blk8, 0, hc.shape[0] // 8 - 1), col * nc + j)
        return pl.BlockSpec((8, tc), imap)

    kern = functools.partial(_conv_kernel, ts=ts)
    return pl.pallas_call(
        kern,
        grid=(bsz, nt, nc),
        in_specs=[pl.BlockSpec((C_WIDTH, tc), lambda b, n, j: (0, j)),
                  main(0), main(1), main(2), halo(1, -1), halo(2, -1), halo(1, 1), halo(2, 1)],
        out_specs=pl.BlockSpec((ts, tc), lambda b, n, j: (b * nt + n, j)),
        out_shape=jax.ShapeDtypeStruct((bsz * s, c), BF16),
        compiler_params=_cparams(("arbitrary", "arbitrary", "arbitrary")),
        name="gated_conv",
    )(conv_w, hc, hc, hc, hc, hc, hc, hc)


def _dilated_kernel(q_ref, kp_ref, kc_ref, kn_ref, vp_ref, vc_ref, vn_ref, o_ref, lse_ref,
                    *, blk, half):
    n = pl.program_id(2)
    nt = pl.num_programs(2)
    wk = blk + 2 * half
    r = lax.broadcasted_iota(jnp.int32, (blk, wk), 0)
    c = lax.broadcasted_iota(jnp.int32, (blk, wk), 1)
    valid = (c >= r) & (c <= r + 2 * half)
    valid &= (c >= half) | (n > 0)
    valid &= (c < half + blk) | (n < nt - 1)
    lane = lax.broadcasted_iota(jnp.int32, (blk, LANES), 1)
    lse_all = jnp.zeros((blk, LANES), F32)
    per = LANES // D_HEADS_PER_GROUP
    for hh in range(D_HEADS_PER_GROUP):
        hs = slice(hh * HEAD_DIM, (hh + 1) * HEAD_DIM)
        kk = jnp.concatenate([kp_ref[:, hs], kc_ref[:, hs], kn_ref[:, hs]], axis=0)
        vv = jnp.concatenate([vp_ref[:, hs], vc_ref[:, hs], vn_ref[:, hs]], axis=0)
        sc = jnp.where(valid, _dot_t(q_ref[:, hs], kk), NEG_INF)
        m = jnp.max(sc, axis=-1, keepdims=True)
        p = jnp.exp(sc - m)
        l = jnp.sum(p, axis=-1, keepdims=True)
        o_ref[:, hs] = _dot(p.astype(BF16), vv) / l
        lse_all = jnp.where(lane // per == hh, m + jnp.log(l), lse_all)
    lse_ref[...] = lse_all


def dilated_group(h, g, dil, row0, bsz, s):
    t, ncol = h.shape
    blk, half = 2 * D_NK, D_NK
    gw = D_HEADS_PER_GROUP * HEAD_DIM
    ncb = ncol // gw
    n_groups = ncb // 3
    hv = h.reshape(t // dil, dil * ncol)
    sd = s // dil
    nt = sd // blk
    rb0 = row0 // dil // blk
    hb = blk // half

    def qspec(sec):
        return pl.BlockSpec((blk, gw), lambda b, r, n: (rb0 + b * nt + n, r * ncb + sec * n_groups + g))

    def hspec(sec, shift):
        def imap(b, r, n):
            nn = jnp.clip(n * hb + (-1 if shift < 0 else hb), 0, nt * hb - 1)
            return ((rb0 + b * nt) * hb + nn, r * ncb + sec * n_groups + g)
        return pl.BlockSpec((half, gw), imap)

    kern = functools.partial(_dilated_kernel, blk=blk, half=half)
    o, lse = pl.pallas_call(
        kern,
        grid=(bsz, dil, nt),
        in_specs=[qspec(0), hspec(1, -1), qspec(1), hspec(1, 1), hspec(2, -1), qspec(2), hspec(2, 1)],
        out_specs=[pl.BlockSpec((blk, gw), lambda b, r, n: (b * nt + n, r)),
                   pl.BlockSpec((blk, LANES), lambda b, r, n: (b * nt + n, r))],
        out_shape=[jax.ShapeDtypeStruct((bsz * sd, dil * gw), F32),
                   jax.ShapeDtypeStruct((bsz * sd, dil * LANES), F32)],
        compiler_params=_cparams(("arbitrary", "arbitrary", "arbitrary")),
        name="dilated_attn",
    )(hv, hv, hv, hv, hv, hv, hv)
    return o.reshape(bsz * s, gw), lse.reshape(bsz * s, LANES)


def _dil_combine_kernel(o0_ref, o1_ref, o2_ref, l0_ref, l1_ref, l2_ref, out_ref):
    per = LANES // D_HEADS_PER_GROUP
    for hh in range(D_HEADS_PER_GROUP):
        hs = slice(hh * HEAD_DIM, (hh + 1) * HEAD_DIM)
        ls = slice(hh * per, hh * per + 1)
        a0, a1, a2 = l0_ref[:, ls], l1_ref[:, ls], l2_ref[:, ls]
        m = jnp.maximum(jnp.maximum(a0, a1), a2)
        w0, w1, w2 = jnp.exp(a0 - m), jnp.exp(a1 - m), jnp.exp(a2 - m)
        den = w0 + w1 + w2
        o = (w0 / den) * o0_ref[:, hs] + (w1 / den) * o1_ref[:, hs] + (w2 / den) * o2_ref[:, hs]
        out_ref[:, hs] = o.astype(out_ref.dtype)


def dilated_combine(os_, lses, *, tm):
    m, gw = os_[0].shape
    ospec = pl.BlockSpec((tm, gw), lambda i: (i, 0))
    lspec = pl.BlockSpec((tm, LANES), lambda i: (i, 0))
    return pl.pallas_call(
        _dil_combine_kernel,
        grid=(m // tm,),
        in_specs=[ospec, ospec, ospec, lspec, lspec, lspec],
        out_specs=ospec,
        out_shape=jax.ShapeDtypeStruct((m, gw), BF16),
        compiler_params=_cparams(("arbitrary",)),
        name="dilated_combine",
    )(*os_, *lses)


def _outproj_kernel(*refs, n_in, alpha):
    f_refs = refs[:n_in]
    w_refs = refs[n_in:2 * n_in]
    x_ref, g_ref, b_ref, o_ref, ob_ref = refs[2 * n_in:]
    acc = alpha * x_ref[...]
    for f_ref, w_ref in zip(f_refs, w_refs):
        acc = acc + _dot(f_ref[...], w_ref[...])
    y = _layer_norm(acc, g_ref[...], b_ref[...])
    o_ref[...] = y
    ob_ref[...] = y.astype(BF16)


def outproj_ln(fs, w, x, g, b, *, alpha, tm):
    m, d = x.shape
    n_in = len(fs)
    widths = [f.shape[1] for f in fs]
    in_specs = [pl.BlockSpec((tm, wd), lambda i: (i, 0)) for wd in widths]
    ws = []
    off = 0
    for wd in widths:
        assert off % wd == 0
        in_specs.append(pl.BlockSpec((wd, d), functools.partial(lambda i, o: (o, 0), o=off // wd)))
        ws.append(w)
        off += wd
    row = pl.BlockSpec((1, d), lambda i: (0, 0))
    in_specs += [pl.BlockSpec((tm, d), lambda i: (i, 0)), row, row]
    kern = functools.partial(_outproj_kernel, n_in=n_in, alpha=alpha)
    return pl.pallas_call(
        kern,
        grid=(m // tm,),
        in_specs=in_specs,
        out_specs=[pl.BlockSpec((tm, d), lambda i: (i, 0)), pl.BlockSpec((tm, d), lambda i: (i, 0))],
        out_shape=[jax.ShapeDtypeStruct((m, d), F32), jax.ShapeDtypeStruct((m, d), BF16)],
        compiler_params=_cparams(("arbitrary",)),
        name="outproj_ln",
    )(*fs, *ws, x, g, b)


def _mem_kernel(xb_ref, x_ref, kv_ref, wq_ref, wo_ref, g_ref, b_ref, rw_ref, rb_ref,
                o_ref, ob_ref, lg_ref, *, alpha):
    inner = wq_ref.shape[1]
    hd = inner // MEM_HEADS
    q = (_dot(xb_ref[...], wq_ref[...]) * hd ** -0.5).astype(BF16)
    outs = []
    for hh in range(MEM_HEADS):
        hs = slice(hh * hd, (hh + 1) * hd)
        sc = _dot_t(q[:, hs], kv_ref[:, hs])
        m = jnp.max(sc, axis=-1, keepdims=True)
        p = jnp.exp(sc - m)
        l = jnp.sum(p, axis=-1, keepdims=True)
        outs.append(_dot(p.astype(BF16), kv_ref[:, inner + hh * hd:inner + (hh + 1) * hd]) / l)
    o = jnp.concatenate(outs, axis=1).astype(BF16)
    y = _layer_norm(alpha * x_ref[...] + _dot(o, wo_ref[...]), g_ref[...], b_ref[...])
    o_ref[...] = y
    yb = y.astype(BF16)
    ob_ref[...] = yb
    yl = (y - yb.astype(F32)).astype(BF16)
    rw = rw_ref[...]
    rwh = rw.astype(BF16)
    rwl = (rw - rwh.astype(F32)).astype(BF16)
    lg_ref[...] = _dot(yb, rwh) + _dot(yl, rwh) + _dot(yb, rwl) + rb_ref[...]


def mem_attention_ln(xb, x, kv, wq, wo, g, b, rw, rb, segs, *, alpha, tm, n_mem):
    m, d = x.shape
    inner = wq.shape[1]
    full = lambda a: pl.BlockSpec(a.shape, lambda i: (0, 0))
    tile = pl.BlockSpec((tm, d), lambda i: (i, 0))
    kern = functools.partial(_mem_kernel, alpha=alpha)
    return pl.pallas_call(
        kern,
        grid=(m // tm,),
        in_specs=[tile, tile,
                  pl.BlockSpec((n_mem, 2 * inner), lambda i: (segs.batch_of_tile(i, tm), 0)),
                  full(wq), full(wo), full(g), full(b), full(rw), full(rb)],
        out_specs=[tile, tile, pl.BlockSpec((tm, LANES), lambda i: (i, 0))],
        out_shape=[jax.ShapeDtypeStruct((m, d), F32), jax.ShapeDtypeStruct((m, d), BF16),
                   jax.ShapeDtypeStruct((m, LANES), F32)],
        compiler_params=_cparams(("arbitrary",)),
        name="mem_attn_ln",
    )(xb, x, kv, wq, wo, g, b, rw, rb)


def _new_expert(be_ref, i):
    prev = be_ref[jnp.maximum(i - 1, 0)]
    return (i == 0) | (be_ref[i] != prev)


def _moe_gu_kernel(be_ref, nu_ref, x_ref, wg_ref, wu_ref, bg_ref, bu_ref, a_ref, wgb, wub):
    i = pl.program_id(1)

    @pl.when(_new_expert(be_ref, i))
    def _():
        wgb[...] = wg_ref[...].astype(BF16)
        wub[...] = wu_ref[...].astype(BF16)

    @pl.when(i < nu_ref[0])
    def _():
        x = x_ref[...]
        gate = jnp.minimum(_dot(x, wgb[...]) + bg_ref[...], SWIGLU_LIMIT)
        up = jnp.clip(_dot(x, wub[...]) + bu_ref[...], -SWIGLU_LIMIT, SWIGLU_LIMIT)
        sig = 1.0 / (1.0 + jnp.exp(-SWIGLU_ALPHA * gate))
        a_ref[...] = (gate * sig * (up + 1.0)).astype(a_ref.dtype)

    @pl.when(i >= nu_ref[0])
    def _():
        a_ref[...] = jnp.zeros_like(a_ref)


def moe_gate_up(xs, w_gu, b_gu, layer, blk_e, n_used, *, tm, tn):
    p, d = xs.shape
    f = w_gu.shape[3] // 2
    nj = f // tn
    wspec = lambda off: pl.BlockSpec((None, None, d, tn), lambda j, i, be, nu: (layer, be[i], 0, off + j))
    bspec = lambda off: pl.BlockSpec((None, None, 1, tn), lambda j, i, be, nu: (layer, be[i], 0, off + j))
    return pl.pallas_call(
        _moe_gu_kernel,
        grid_spec=pltpu.PrefetchScalarGridSpec(
            num_scalar_prefetch=2,
            grid=(nj, p // tm),
            in_specs=[pl.BlockSpec((tm, d), lambda j, i, be, nu: (i, 0)),
                      wspec(0), wspec(nj), bspec(0), bspec(nj)],
            out_specs=pl.BlockSpec((tm, tn), lambda j, i, be, nu: (i, j)),
            scratch_shapes=[pltpu.VMEM((d, tn), BF16), pltpu.VMEM((d, tn), BF16)],
        ),
        out_shape=jax.ShapeDtypeStruct((p, f), BF16),
        compiler_params=_cparams(("arbitrary", "arbitrary")),
        name="moe_gate_up",
    )(blk_e, n_used, xs, w_gu, w_gu, b_gu, b_gu)


def _moe_down_kernel(be_ref, nu_ref, a_ref, w_ref, b_ref, y_ref, wb):
    i = pl.program_id(1)

    @pl.when(_new_expert(be_ref, i))
    def _():
        wb[...] = w_ref[...].astype(BF16)

    @pl.when(i < nu_ref[0])
    def _():
        y_ref[...] = _dot(a_ref[...], wb[...]) + b_ref[...]

    @pl.when(i >= nu_ref[0])
    def _():
        y_ref[...] = jnp.zeros_like(y_ref)


def moe_down(a, w_down, b_down, layer, blk_e, n_used, *, tm, tn):
    p, f = a.shape
    d = w_down.shape[3]
    return pl.pallas_call(
        _moe_down_kernel,
        grid_spec=pltpu.PrefetchScalarGridSpec(
            num_scalar_prefetch=2,
            grid=(d // tn, p // tm),
            in_specs=[pl.BlockSpec((tm, f), lambda j, i, be, nu: (i, 0)),
                      pl.BlockSpec((None, None, f, tn), lambda j, i, be, nu: (layer, be[i], 0, j)),
                      pl.BlockSpec((None, None, 1, tn), lambda j, i, be, nu: (layer, be[i], 0, j))],
            out_specs=pl.BlockSpec((tm, tn), lambda j, i, be, nu: (i, j)),
            scratch_shapes=[pltpu.VMEM((f, tn), BF16)],
        ),
        out_shape=jax.ShapeDtypeStruct((p, d), F32),
        compiler_params=_cparams(("arbitrary", "arbitrary")),
        name="moe_down",
    )(blk_e, n_used, a, w_down, b_down)


def _residual_ln_kernel(x_ref, f_ref, g_ref, b_ref, o_ref, ob_ref, *, alpha):
    y = _layer_norm(alpha * x_ref[...] + f_ref[...], g_ref[...], b_ref[...])
    o_ref[...] = y
    ob_ref[...] = y.astype(BF16)


def residual_ln(x, f, g, b, *, alpha, tm):
    m, d = x.shape
    tile = pl.BlockSpec((tm, d), lambda i: (i, 0))
    row = pl.BlockSpec((1, d), lambda i: (0, 0))
    return pl.pallas_call(
        functools.partial(_residual_ln_kernel, alpha=alpha),
        grid=(m // tm,),
        in_specs=[tile, tile, row, row],
        out_specs=[tile, tile],
        out_shape=[jax.ShapeDtypeStruct((m, d), F32), jax.ShapeDtypeStruct((m, d), BF16)],
        compiler_params=_cparams(("arbitrary",)),
        name="residual_ln",
    )(x, f, g, b)


def route(logits, n_experts, tm):
    t = logits.shape[0]
    a = t * TOP_K
    top_v, top_i = lax.top_k(logits, TOP_K)
    gates = jax.nn.softmax(top_v, axis=-1)
    flat_e = top_i.reshape(-1).astype(jnp.int32)
    order = jnp.argsort(flat_e).astype(jnp.int32)
    sorted_e = flat_e[order]
    counts = jnp.bincount(flat_e, length=n_experts).astype(jnp.int32)
    starts = jnp.cumsum(counts) - counts
    pcounts = (counts + tm - 1) // tm * tm
    pends = jnp.cumsum(pcounts)
    pstarts = pends - pcounts
    dest = pstarts[sorted_e] + jnp.arange(a, dtype=jnp.int32) - starts[sorted_e]
    n_blocks = -(-(a + n_experts * (tm - 1)) // tm)
    row_tok = jnp.zeros((n_blocks * tm,), jnp.int32).at[dest].set(order // TOP_K)
    pos = jnp.zeros((a,), jnp.int32).at[order].set(dest).reshape(t, TOP_K)
    blk_e = jnp.minimum(jnp.searchsorted(pends, jnp.arange(n_blocks, dtype=jnp.int32) * tm, side='right'),
                        n_experts - 1).astype(jnp.int32)
    n_used = (pends[-1] // tm).astype(jnp.int32).reshape(1)
    return gates, row_tok, pos, blk_e, n_used


def kernel(x_prompt, x_sample, mem_prompt, mem_sample, ev_w_in, ev_w_out, ev_sink, ev_lam_q1, ev_lam_k1, ev_lam_q2, ev_lam_k2, ev_dnorm_w, od_w_in, od_w_out, od_conv_w, mem_wq, mem_wkv, mem_wo, ln_g, ln_b, moe_router_w, moe_router_b, moe_w_gu, moe_b_gu, moe_w_down, moe_b_down):
    b1, s1, d = x_prompt.shape
    b2, s2, _ = x_sample.shape
    segs = Segs(b1, s1, b2, s2)
    depth = ln_g.shape[0]
    n_mem = mem_prompt.shape[1]
    n_experts = moe_router_w.shape[2]
    alpha = (2 * depth) ** 0.25
    n_heads = d // (2 * HEAD_DIM)
    n_kv = n_heads // 4
    c_ch = d // 2
    dqkv = len(D_PATTERNS) * D_HEADS_PER_GROUP * HEAD_DIM

    x = jnp.concatenate([x_prompt.reshape(-1, d), x_sample.reshape(-1, d)], axis=0)
    xb = x.astype(BF16)
    memb = jnp.concatenate([mem_prompt.reshape(-1, d), mem_sample.reshape(-1, d)], axis=0).astype(BF16)
    tables = rope_tables(max(s1, s2))
    ev_modes = jnp.array([MODE_ROPE128_Q] * n_heads + [MODE_ROPE128] * n_kv + [MODE_NONE] * n_kv
                         + [MODE_ROPE64_Q] * n_heads + [MODE_ROPE64] * n_heads
                         + [MODE_NONE] * n_heads, jnp.int32)
    nd = dqkv // LANES
    nc = c_ch // LANES
    od_modes = jnp.array([MODE_NONE] * (3 * nc) + [MODE_ROPE128_Q] * nd + [MODE_ROPE128] * nd
                         + [MODE_NONE] * nd, jnp.int32)
    mem_modes = jnp.zeros((mem_wkv.shape[2] // LANES,), jnp.int32)
    mem_segs = Segs(b1, n_mem, b2, n_mem)
    rw_pad = jnp.pad(moe_router_w, ((0, 0), (0, 0), (0, LANES - n_experts)))
    rb_pad = jnp.pad(moe_router_b, ((0, 0), (0, LANES - n_experts)))[:, None, :]
    b_gu4 = moe_b_gu[:, :, None, :]
    b_down4 = moe_b_down[:, :, None, :]
    moe_tm = 512

    for l in range(depth):
        j = l // 2
        if l % 2 == 0:
            lambda_init = 0.8 - 0.6 * math.exp(-0.3 * l)
            w_in = ev_w_in[j].astype(BF16)
            n_in = w_in.shape[1]
            tn = n_in // 6 if (n_in // 6) % LANES == 0 else n_in // 18
            h = proj(xb, w_in, ev_modes, tables, segs, col0=0, ncols=n_in, tm=512, tn=tn,
                     out_dtype=BF16)
            lamv = jnp.stack([ev_lam_q1[j], ev_lam_k1[j], ev_lam_q2[j], ev_lam_k2[j]])
            oa, ob = [], []
            for row0, bsz, s, _ in segs.groups():
                oa.append(window_attention(h, ev_sink[j], row0, bsz, s, d_model=d))
                ob.append(diff_attention(h, lamv, ev_dnorm_w[j][None, :], row0, bsz, s, lambda_init,
                                         d_model=d, tq=256, tk=512))
            fs = [jnp.concatenate(oa, axis=0), jnp.concatenate(ob, axis=0)]
            w_out = ev_w_out[j].astype(BF16)
        else:
            w_in = od_w_in[j].astype(BF16)
            hc = proj(xb, w_in, od_modes, tables, segs, col0=0, ncols=3 * c_ch, tm=512,
                      tn=c_ch, out_dtype=F32)
            hd = proj(xb, w_in, od_modes, tables, segs, col0=3 * c_ch, ncols=3 * dqkv, tm=512,
                      tn=dqkv // 2 if (3 * c_ch) % dqkv else dqkv, out_dtype=BF16)
            oc, od = [], []
            for row0, bsz, s, _ in segs.groups():
                oc.append(gated_conv(hc, od_conv_w[j], row0, bsz, s, ts=512, tc=c_ch // 2))
                os_, lses = [], []
                for g, (_, dil) in enumerate(D_PATTERNS):
                    o_g, lse_g = dilated_group(hd, g, dil, row0, bsz, s)
                    os_.append(o_g)
                    lses.append(lse_g)
                od.append(dilated_combine(os_, lses, tm=512))
            fs = [jnp.concatenate(oc, axis=0), jnp.concatenate(od, axis=0)]
            w_out = od_w_out[j].astype(BF16)
        x, xb = outproj_ln(fs, w_out, x, ln_g[l, 0][None, :], ln_b[l, 0][None, :], alpha=alpha, tm=256)

        kv = proj(memb, mem_wkv[l].astype(BF16), mem_modes, tables, mem_segs, col0=0,
                  ncols=mem_wkv.shape[2], tm=n_mem, tn=mem_wkv.shape[2] // 2, out_dtype=BF16)
        x, xb, logits = mem_attention_ln(xb, x, kv, mem_wq[l].astype(BF16), mem_wo[l].astype(BF16),
                                         ln_g[l, 1][None, :], ln_b[l, 1][None, :], rw_pad[l], rb_pad[l],
                                         segs, alpha=alpha, tm=256, n_mem=n_mem)

        gates, row_tok, pos, blk_e, n_used = route(logits[:, :n_experts], n_experts, moe_tm)
        xs = jnp.take(xb, row_tok, axis=0)
        a = moe_gate_up(xs, moe_w_gu, b_gu4, l, blk_e, n_used, tm=moe_tm, tn=512)
        ys = moe_down(a, moe_w_down, b_down4, l, blk_e, n_used, tm=moe_tm, tn=512)
        f = jnp.sum(jnp.take(ys, pos, axis=0) * gates[:, :, None], axis=1)
        x, xb = residual_ln(x, f, ln_g[l, 2][None, :], ln_b[l, 2][None, :], alpha=alpha, tm=512)

    y1 = x[:segs.t1].reshape(b1, s1, d)
    y2 = x[segs.t1:].reshape(b2, s2, d)
    return (y1, y2)
```

```python
import functools
import math

import jax
import jax.numpy as jnp
from jax import lax
from jax.experimental import pallas as pl
from jax.experimental.pallas import tpu as pltpu

F32 = jnp.float32
BF16 = jnp.bfloat16
U32 = jnp.uint32
LANES = 128
SUBLANES = 8
VMEM_LIMIT = 56 * 1024 * 1024

HEAD_DIM = 128
ROPE_THETA = 10000.0
A_WINDOW = 128
B_DIM = 64
C_WIDTH = 3
D_PATTERNS = ((128, 1), (512, 4), (2048, 16))
D_HEADS_PER_GROUP = 4
D_NK = 64
MEM_HEADS = 4
TOP_K = 4
SWIGLU_LIMIT = 7.0
SWIGLU_ALPHA = 1.702
LN_EPS = 1e-5
NEG_INF = -1e30

MODE_NONE, MODE_ROPE128, MODE_ROPE128_Q, MODE_ROPE64, MODE_ROPE64_Q = 0, 1, 2, 3, 4


def _cparams(sem):
    return pltpu.CompilerParams(dimension_semantics=sem, vmem_limit_bytes=VMEM_LIMIT)


def _dot(a, b):
    return jnp.dot(a, b, preferred_element_type=F32)


def _dot_t(a, b):
    return lax.dot_general(a, b, (((1,), (1,)), ((), ())), preferred_element_type=F32)


class Segs:
    def __init__(self, b1, s1, b2, s2):
        self.b1, self.s1, self.b2, self.s2 = b1, s1, b2, s2
        self.t1 = b1 * s1
        self.t = b1 * s1 + b2 * s2

    def groups(self):
        return ((0, self.b1, self.s1), (self.t1, self.b2, self.s2))

    def pos_tile(self, i, tm):
        n1 = self.t1 // tm
        return jnp.where(i < n1, i % (self.s1 // tm), (i - n1) % (self.s2 // tm))

    def batch_of_tile(self, i, tm):
        n1 = self.t1 // tm
        return jnp.where(i < n1, (i * tm) // self.s1, self.b1 + ((i - n1) * tm) // self.s2)


def _layer_norm(z, g, b):
    mu = jnp.mean(z, axis=-1, keepdims=True)
    zc = z - mu
    var = jnp.mean(zc * zc, axis=-1, keepdims=True)
    return zc * lax.rsqrt(var + LN_EPS) * g + b


def _rope128(a, c, s):
    return a * c + pltpu.roll(a, 64, 1) * s


def _proj_kernel(mode_ref, x_ref, w_ref, c128_ref, s128_ref, c64_ref, sa64_ref, sb64_ref, o_ref,
                 *, n_slabs, slab0, q128_scale, q64_scale):
    j = pl.program_id(0)
    acc = _dot(x_ref[...], w_ref[...])
    for s in range(n_slabs):
        a = acc[:, s * LANES:(s + 1) * LANES]
        mode = mode_ref[slab0 + j * n_slabs + s]
        cols = slice(s * LANES, (s + 1) * LANES)

        @pl.when(mode == MODE_NONE)
        def _():
            o_ref[:, cols] = a.astype(o_ref.dtype)

        def rope128(scale):
            r = _rope128(a, c128_ref[...], s128_ref[...])
            o_ref[:, cols] = (r * scale).astype(o_ref.dtype)

        def rope64(scale):
            r = (a * c64_ref[...] + pltpu.roll(a, 96, 1) * sa64_ref[...]
                 + pltpu.roll(a, 32, 1) * sb64_ref[...])
            o_ref[:, cols] = (r * scale).astype(o_ref.dtype)

        pl.when(mode == MODE_ROPE128)(functools.partial(rope128, 1.0))
        pl.when(mode == MODE_ROPE128_Q)(functools.partial(rope128, q128_scale))
        pl.when(mode == MODE_ROPE64)(functools.partial(rope64, 1.0))
        pl.when(mode == MODE_ROPE64_Q)(functools.partial(rope64, q64_scale))


def proj(xb, w, modes, tables, segs, *, col0, ncols, tm, tn, out_dtype):
    m, k = xb.shape
    assert ncols % tn == 0 and col0 % tn == 0 and m % tm == 0 and tn % LANES == 0
    n_slabs = tn // LANES
    jb = col0 // tn
    tab_spec = pl.BlockSpec((tm, LANES), lambda j, i, *_: (segs.pos_tile(i, tm), 0))
    kern = functools.partial(_proj_kernel, n_slabs=n_slabs, slab0=col0 // LANES,
                             q128_scale=HEAD_DIM ** -0.5, q64_scale=B_DIM ** -0.5)
    return pl.pallas_call(
        kern,
        grid_spec=pltpu.PrefetchScalarGridSpec(
            num_scalar_prefetch=1,
            grid=(ncols // tn, m // tm),
            in_specs=[pl.BlockSpec((tm, k), lambda j, i, *_: (i, 0)),
                      pl.BlockSpec((k, tn), lambda j, i, *_: (0, jb + j)),
                      tab_spec, tab_spec, tab_spec, tab_spec, tab_spec],
            out_specs=pl.BlockSpec((tm, tn), lambda j, i, *_: (i, j)),
        ),
        out_shape=jax.ShapeDtypeStruct((m, ncols), out_dtype),
        compiler_params=_cparams(("arbitrary", "arbitrary")),
        name="proj",
    )(modes, xb, w, *tables)


def rope_tables(s):
    inv = ROPE_THETA ** (-jnp.arange(0, HEAD_DIM, 2, dtype=F32) / HEAD_DIM)
    ang = jnp.arange(s, dtype=F32)[:, None] * inv[None, :]
    c, sn = jnp.cos(ang), jnp.sin(ang)
    c128 = jnp.concatenate([c, c], axis=1)
    s128 = jnp.concatenate([-sn, sn], axis=1)
    invb = ROPE_THETA ** (-jnp.arange(0, B_DIM, 2, dtype=F32) / B_DIM)
    angb = jnp.arange(s, dtype=F32)[:, None] * invb[None, :]
    cb, sb = jnp.cos(angb), jnp.sin(angb)
    z = jnp.zeros_like(sb)
    c64 = jnp.concatenate([cb, cb, cb, cb], axis=1)
    sa64 = jnp.concatenate([-sb, z, -sb, z], axis=1)
    sb64 = jnp.concatenate([z, sb, z, sb], axis=1)
    return (c128, s128, c64, sa64, sb64)


def _window_kernel(sink_ref, q_ref, kp_ref, kc_ref, kn_ref, vp_ref, vc_ref, vn_ref, o_ref,
                   *, n_heads, n_kv, blk):
    n = pl.program_id(1)
    nb = pl.num_programs(1)
    r = lax.broadcasted_iota(jnp.int32, (blk, 3 * blk), 0)
    c = lax.broadcasted_iota(jnp.int32, (blk, 3 * blk), 1)
    valid = jnp.abs(r + blk - c) <= A_WINDOW
    valid &= (c >= blk) | (n > 0)
    valid &= (c < 2 * blk) | (n < nb - 1)
    g = n_heads // n_kv
    for kv in range(n_kv):
        cs = slice(kv * HEAD_DIM, (kv + 1) * HEAD_DIM)
        k3 = jnp.concatenate([kp_ref[:, cs], kc_ref[:, cs], kn_ref[:, cs]], axis=0)
        v3 = jnp.concatenate([vp_ref[:, cs], vc_ref[:, cs], vn_ref[:, cs]], axis=0)
        for gi in range(g):
            h = kv * g + gi
            hs = slice(h * HEAD_DIM, (h + 1) * HEAD_DIM)
            sc = jnp.where(valid, _dot_t(q_ref[:, hs], k3), NEG_INF)
            sk = sink_ref[h]
            m = jnp.maximum(jnp.max(sc, axis=-1, keepdims=True), sk)
            p = jnp.exp(sc - m)
            denom = jnp.sum(p, axis=-1, keepdims=True) + jnp.exp(sk - m)
            o = _dot(p.astype(BF16), v3) / denom
            o_ref[:, hs] = o.astype(o_ref.dtype)


def window_attention(h, sink, row0, bsz, s, *, d_model):
    blk = A_WINDOW
    n_heads = d_model // (2 * HEAD_DIM)
    n_kv = n_heads // 4
    qw, kw = n_heads * HEAD_DIM, n_kv * HEAD_DIM
    nb = s // blk
    rb0 = row0 // blk
    kcol, vcol = qw // kw, qw // kw + 1

    def rows(b, n):
        return rb0 + b * nb + n

    def spec(col, shift):
        def imap(b, n):
            nn = jnp.clip(n + shift, 0, nb - 1)
            return (rows(b, nn), col)
        return pl.BlockSpec((blk, kw), imap)

    kern = functools.partial(_window_kernel, n_heads=n_heads, n_kv=n_kv, blk=blk)
    return pl.pallas_call(
        kern,
        grid=(bsz, nb),
        in_specs=[pl.BlockSpec(memory_space=pltpu.SMEM),
                  pl.BlockSpec((blk, qw), lambda b, n: (rows(b, n), 0)),
                  spec(kcol, -1), spec(kcol, 0), spec(kcol, 1),
                  spec(vcol, -1), spec(vcol, 0), spec(vcol, 1)],
        out_specs=pl.BlockSpec((blk, qw), lambda b, n: (b * nb + n, 0)),
        out_shape=jax.ShapeDtypeStruct((bsz * s, qw), BF16),
        compiler_params=_cparams(("arbitrary", "arbitrary")),
        name="window_attn",
    )(sink, h, h, h, h, h, h, h)


def _diff_kernel(lamv_ref, nw_ref, q_ref, k_ref, v_ref, o_ref, *, tq, tk, n_streams, lambda_init):
    s = k_ref.shape[0]
    ts = tq // n_streams
    lane = lax.broadcasted_iota(jnp.int32, (ts, LANES), 1)
    q2s = []
    for st in range(n_streams):
        q = q_ref[st * ts:(st + 1) * ts, :]
        zero = jnp.zeros_like(q)
        q2s.append(jnp.concatenate([jnp.where(lane < B_DIM, q, zero),
                                    jnp.where(lane >= B_DIM, q, zero)], axis=0))

    def body(i, carry):
        ks = pl.multiple_of(i * tk, tk)
        kb = k_ref[pl.ds(ks, tk), :]
        vb = v_ref[pl.ds(ks, tk), :]
        out = []
        for st in range(n_streams):
            m, l, acc = carry[st]
            sc = _dot_t(q2s[st], kb)
            m_new = jnp.maximum(m, jnp.max(sc, axis=-1, keepdims=True))
            a = jnp.exp(m - m_new)
            p = jnp.exp(sc - m_new)
            l = a * l + jnp.sum(p, axis=-1, keepdims=True)
            acc = a * acc + _dot(p.astype(BF16), vb)
            out.append((m_new, l, acc))
        return tuple(out)

    init = tuple((jnp.full((2 * ts, 1), NEG_INF, F32), jnp.zeros((2 * ts, 1), F32),
                  jnp.zeros((2 * ts, LANES), F32)) for _ in range(n_streams))
    res = lax.fori_loop(0, s // tk, body, init)
    lv = lamv_ref[...]
    lam = (jnp.exp(jnp.sum(lv[0:1] * lv[1:2], axis=-1, keepdims=True))
           - jnp.exp(jnp.sum(lv[2:3] * lv[3:4], axis=-1, keepdims=True)) + lambda_init)
    for st in range(n_streams):
        m, l, acc = res[st]
        on = acc / l
        of = on[:ts] - lam * on[ts:]
        of = of * lax.rsqrt(jnp.mean(of * of, axis=-1, keepdims=True) + LN_EPS) * nw_ref[...]
        o_ref[st * ts:(st + 1) * ts, :] = (of * (1.0 - lambda_init)).astype(o_ref.dtype)


def diff_attention(h, lamv, norm_w, row0, bsz, s, lambda_init, *, d_model, tq, tk, n_streams):
    n_heads = d_model // (2 * HEAD_DIM)
    a_cols = (n_heads + 2 * (n_heads // 4))
    qc, kc, vc = a_cols, a_cols + n_heads, a_cols + 2 * n_heads
    nq = s // tq
    assert row0 % s == 0
    sb0 = row0 // s
    qb0 = row0 // tq
    kern = functools.partial(_diff_kernel, tq=tq, tk=tk, n_streams=n_streams, lambda_init=lambda_init)
    return pl.pallas_call(
        kern,
        grid=(bsz, n_heads, nq),
        in_specs=[pl.BlockSpec((4, B_DIM), lambda b, hh, n: (0, 0)),
                  pl.BlockSpec((1, LANES), lambda b, hh, n: (0, 0)),
                  pl.BlockSpec((tq, LANES), lambda b, hh, n: (qb0 + b * nq + n, qc + hh)),
                  pl.BlockSpec((s, LANES), lambda b, hh, n: (sb0 + b, kc + hh)),
                  pl.BlockSpec((s, LANES), lambda b, hh, n: (sb0 + b, vc + hh))],
        out_specs=pl.BlockSpec((tq, LANES), lambda b, hh, n: (b * nq + n, hh)),
        out_shape=jax.ShapeDtypeStruct((bsz * s, n_heads * LANES), BF16),
        compiler_params=_cparams(("arbitrary", "arbitrary", "arbitrary")),
        name="diff_attn",
    )(lamv, norm_w, h, h, h)


def _conv_kernel(cw_ref, b_ref, c_ref, x_ref, cp_ref, xp_ref, cn_ref, xn_ref, o_ref, *, ts):
    n = pl.program_id(1)
    nt = pl.num_programs(1)
    u = c_ref[...] * x_ref[...]
    prev = jnp.where(n > 0, cp_ref[7:8, :] * xp_ref[7:8, :], 0.0)
    nxt = jnp.where(n < nt - 1, cn_ref[0:1, :] * xn_ref[0:1, :], 0.0)
    row = lax.broadcasted_iota(jnp.int32, u.shape, 0)
    um1 = jnp.where(row == 0, prev, pltpu.roll(u, 1, 0))
    up1 = jnp.where(row == ts - 1, nxt, pltpu.roll(u, ts - 1, 0))
    conv = um1 * cw_ref[0:1, :] + u * cw_ref[1:2, :] + up1 * cw_ref[2:3, :]
    o_ref[...] = (b_ref[...] * conv).astype(o_ref.dtype)


def gated_conv(hc, conv_w, row0, bsz, s, *, ts, tc):
    c = hc.shape[1] // 3
    nt = s // ts
    nc = c // tc
    rb0 = row0 // ts
    r8 = ts // 8

    def main(col):
        return pl.BlockSpec((ts, tc), lambda b, n, j: (rb0 + b * nt + n, col * nc + j))

    def halo(col, shift):
        def imap(b, n, j):
            blk8 = (rb0 + b * nt + n) * r8 + (-1 if shift < 0 else r8)
            return (jnp.clip(blk8, 0, hc.shape[0] // 8 - 1), col * nc + j)
        return pl.BlockSpec((8, tc), imap)

    kern = functools.partial(_conv_kernel, ts=ts)
    return pl.pallas_call(
        kern,
        grid=(bsz, nt, nc),
        in_specs=[pl.BlockSpec((C_WIDTH, tc), lambda b, n, j: (0, j)),
                  main(0), main(1), main(2), halo(1, -1), halo(2, -1), halo(1, 1), halo(2, 1)],
        out_specs=pl.BlockSpec((ts, tc), lambda b, n, j: (b * nt + n, j)),
        out_shape=jax.ShapeDtypeStruct((bsz * s, c), BF16),
        compiler_params=_cparams(("arbitrary", "arbitrary", "arbitrary")),
        name="gated_conv",
    )(conv_w, hc, hc, hc, hc, hc, hc, hc)


def _proj_dil_kernel(x_ref, w_ref, c_ref, s_ref, o_ref, *scrs, dil, tm, q_scale):
    sec = pl.program_id(0)
    rows = tm // dil
    acc = _dot(x_ref[...], w_ref[...])
    n_slabs = len(scrs)
    for sl, scr in enumerate(scrs):
        scr[...] = acc[:, sl * LANES:(sl + 1) * LANES]

    def emit(rope, scale):
        for r in range(dil):
            rs = pl.ds(r, rows, stride=dil) if dil > 1 else pl.ds(0, rows)
            for sl in range(n_slabs):
                cols = slice(sl * LANES, (sl + 1) * LANES)
                a = scrs[sl][rs, :]
                if rope:
                    a = _rope128(a, c_ref[rs, :], s_ref[rs, :]) * scale
                o_ref[r, :, cols] = a.astype(o_ref.dtype)

    pl.when(sec == 0)(functools.partial(emit, True, q_scale))
    pl.when(sec == 1)(functools.partial(emit, True, 1.0))
    pl.when(sec == 2)(functools.partial(emit, False, 1.0))


def proj_dilated(xb, w, tables, segs, g, dil, *, col0, tm):
    m, k = xb.shape
    gw = D_HEADS_PER_GROUP * HEAD_DIM
    n_groups = len(D_PATTERNS)
    cb0 = col0 // gw
    tab = pl.BlockSpec((tm, LANES), lambda j, i: (segs.pos_tile(i, tm), 0))
    kern = functools.partial(_proj_dil_kernel, dil=dil, tm=tm, q_scale=HEAD_DIM ** -0.5)
    return pl.pallas_call(
        kern,
        grid=(3, m // tm),
        in_specs=[pl.BlockSpec((tm, k), lambda j, i: (i, 0)),
                  pl.BlockSpec((k, gw), lambda j, i: (0, cb0 + j * n_groups + g)),
                  tab, tab],
        out_specs=pl.BlockSpec((dil, tm // dil, gw), lambda j, i: (0, i, j)),
        out_shape=jax.ShapeDtypeStruct((dil, m // dil, 3 * gw), BF16),
        scratch_shapes=[pltpu.VMEM((tm, LANES), F32) for _ in range(gw // LANES)],
        compiler_params=_cparams(("arbitrary", "arbitrary")),
        name="proj_dilated",
    )(xb, w, tables[0], tables[1])


def _dilated_kernel(q_ref, kp_ref, kc_ref, kn_ref, vp_ref, vc_ref, vn_ref, o_ref, lse_ref,
                    *, blk, half):
    n = pl.program_id(2)
    nt = pl.num_programs(2)
    wk = blk + 2 * half
    r = lax.broadcasted_iota(jnp.int32, (blk, wk), 0)
    c = lax.broadcasted_iota(jnp.int32, (blk, wk), 1)
    valid = (c >= r) & (c <= r + 2 * half)
    valid &= (c >= half) | (n > 0)
    valid &= (c < half + blk) | (n < nt - 1)
    lane = lax.broadcasted_iota(jnp.int32, (blk, LANES), 1)
    lse_all = jnp.zeros((blk, LANES), F32)
    per = LANES // D_HEADS_PER_GROUP
    for hh in range(D_HEADS_PER_GROUP):
        hs = slice(hh * HEAD_DIM, (hh + 1) * HEAD_DIM)
        kk = jnp.concatenate([kp_ref[:, hs], kc_ref[:, hs], kn_ref[:, hs]], axis=0)
        vv = jnp.concatenate([vp_ref[:, hs], vc_ref[:, hs], vn_ref[:, hs]], axis=0)
        sc = jnp.where(valid, _dot_t(q_ref[:, hs], kk), NEG_INF)
        m = jnp.max(sc, axis=-1, keepdims=True)
        p = jnp.exp(sc - m)
        l = jnp.sum(p, axis=-1, keepdims=True)
        o_ref[:, hs] = _dot(p.astype(BF16), vv) / l
        lse_all = jnp.where(lane // per == hh, m + jnp.log(l), lse_all)
    lse_ref[...] = lse_all


def dilated_group(hq, dil, row0, bsz, s):
    blk, half = 2 * D_NK, D_NK
    gw = D_HEADS_PER_GROUP * HEAD_DIM
    sd = s // dil
    nt = sd // blk
    assert (row0 // dil) % blk == 0 and sd % blk == 0
    rb0 = row0 // dil // blk
    hb = blk // half

    def cur(sec):
        return pl.BlockSpec((None, blk, gw), lambda b, r, n: (r, rb0 + b * nt + n, sec))

    def halo(sec, shift):
        def imap(b, r, n):
            nn = jnp.clip(n * hb + (-1 if shift < 0 else hb), 0, nt * hb - 1)
            return (r, (rb0 + b * nt) * hb + nn, sec)
        return pl.BlockSpec((None, half, gw), imap)

    kern = functools.partial(_dilated_kernel, blk=blk, half=half)
    return pl.pallas_call(
        kern,
        grid=(bsz, dil, nt),
        in_specs=[cur(0), halo(1, -1), cur(1), halo(1, 1), halo(2, -1), cur(2), halo(2, 1)],
        out_specs=[pl.BlockSpec((None, blk, gw), lambda b, r, n: (r, b * nt + n, 0)),
                   pl.BlockSpec((None, blk, LANES), lambda b, r, n: (r, b * nt + n, 0))],
        out_shape=[jax.ShapeDtypeStruct((dil, bsz * sd, gw), F32),
                   jax.ShapeDtypeStruct((dil, bsz * sd, LANES), F32)],
        compiler_params=_cparams(("arbitrary", "arbitrary", "arbitrary")),
        name="dilated_attn",
    )(hq, hq, hq, hq, hq, hq, hq)


def _dil_combine_kernel(*refs, dils, tm):
    ng = len(dils)
    nh = D_HEADS_PER_GROUP
    o_refs, l_refs = refs[:ng], refs[ng:2 * ng]
    out_ref = refs[2 * ng]
    scr = refs[2 * ng + 1:]
    o_scr = [scr[g * (nh + 1):g * (nh + 1) + nh] for g in range(ng)]
    l_scr = [scr[g * (nh + 1) + nh] for g in range(ng)]
    for o_ref, l_ref, osc, lsc, dil in zip(o_refs, l_refs, o_scr, l_scr, dils):
        for r in range(dil):
            rs = pl.ds(r, tm // dil, stride=dil) if dil > 1 else pl.ds(0, tm)
            for hh in range(nh):
                osc[hh][rs, :] = o_ref[r, :, hh * HEAD_DIM:(hh + 1) * HEAD_DIM]
            lsc[rs, :] = l_ref[r]
    per = LANES // nh
    for hh in range(nh):
        hs = slice(hh * HEAD_DIM, (hh + 1) * HEAD_DIM)
        ls = slice(hh * per, hh * per + 1)
        lses = [lsc[:, ls] for lsc in l_scr]
        m = functools.reduce(jnp.maximum, lses)
        ws = [jnp.exp(a - m) for a in lses]
        den = functools.reduce(lambda a, b: a + b, ws)
        o = functools.reduce(lambda a, b: a + b, [(w / den) * osc[hh][...] for w, osc in zip(ws, o_scr)])
        out_ref[:, hs] = o.astype(out_ref.dtype)


def dilated_combine(os_, lses, dils, *, tm):
    gw = os_[0].shape[2]
    m = os_[0].shape[0] * os_[0].shape[1]
    ospecs = [pl.BlockSpec((d, tm // d, gw), lambda i: (0, i, 0)) for d in dils]
    lspecs = [pl.BlockSpec((d, tm // d, LANES), lambda i: (0, i, 0)) for d in dils]
    kern = functools.partial(_dil_combine_kernel, dils=dils, tm=tm)
    return pl.pallas_call(
        kern,
        grid=(m // tm,),
        in_specs=ospecs + lspecs,
        out_specs=pl.BlockSpec((tm, gw), lambda i: (i, 0)),
        out_shape=jax.ShapeDtypeStruct((m, gw), BF16),
        scratch_shapes=[pltpu.VMEM((tm, LANES), F32) for _ in range(len(dils) * (D_HEADS_PER_GROUP + 1))],
        compiler_params=_cparams(("arbitrary",)),
        name="dilated_combine",
    )(*os_, *lses)


def _outproj_kernel(*refs, n_in, alpha):
    f_refs = refs[:n_in]
    w_refs = refs[n_in:2 * n_in]
    x_ref, g_ref, b_ref, o_ref, ob_ref = refs[2 * n_in:]
    acc = alpha * x_ref[...]
    for f_ref, w_ref in zip(f_refs, w_refs):
        acc = acc + _dot(f_ref[...], w_ref[...])
    y = _layer_norm(acc, g_ref[...], b_ref[...])
    o_ref[...] = y
    ob_ref[...] = y.astype(BF16)


def outproj_ln(fs, w, x, g, b, *, alpha, tm):
    m, d = x.shape
    n_in = len(fs)
    widths = [f.shape[1] for f in fs]
    in_specs = [pl.BlockSpec((tm, wd), lambda i: (i, 0)) for wd in widths]
    ws = []
    off = 0
    for wd in widths:
        assert off % wd == 0
        in_specs.append(pl.BlockSpec((wd, d), functools.partial(lambda i, o: (o, 0), o=off // wd)))
        ws.append(w)
        off += wd
    row = pl.BlockSpec((1, d), lambda i: (0, 0))
    in_specs += [pl.BlockSpec((tm, d), lambda i: (i, 0)), row, row]
    kern = functools.partial(_outproj_kernel, n_in=n_in, alpha=alpha)
    return pl.pallas_call(
        kern,
        grid=(m // tm,),
        in_specs=in_specs,
        out_specs=[pl.BlockSpec((tm, d), lambda i: (i, 0)), pl.BlockSpec((tm, d), lambda i: (i, 0))],
        out_shape=[jax.ShapeDtypeStruct((m, d), F32), jax.ShapeDtypeStruct((m, d), BF16)],
        compiler_params=_cparams(("arbitrary",)),
        name="outproj_ln",
    )(*fs, *ws, x, g, b)


def _pack_halves(y):
    bits = lax.bitcast_convert_type(y.astype(BF16).astype(F32), U32)
    hw = y.shape[1] // 2
    return (bits[:, hw:] & jnp.uint32(0xFFFF0000)) | (bits[:, :hw] >> 16)


def _unpack_halves(pk):
    lo = lax.bitcast_convert_type(pk << 16, F32).astype(BF16)
    hi = lax.bitcast_convert_type(pk & jnp.uint32(0xFFFF0000), F32).astype(BF16)
    return lo, hi


def _mem_kernel(xb_ref, x_ref, kv_ref, wq_ref, wo_ref, g_ref, b_ref, rw_ref, rb_ref,
                o_ref, pk_ref, lg_ref, *, alpha):
    inner = wq_ref.shape[1]
    hd = inner // MEM_HEADS
    q = (_dot(xb_ref[...], wq_ref[...]) * hd ** -0.5).astype(BF16)
    outs = []
    for hh in range(MEM_HEADS):
        hs = slice(hh * hd, (hh + 1) * hd)
        sc = _dot_t(q[:, hs], kv_ref[:, hs])
        m = jnp.max(sc, axis=-1, keepdims=True)
        p = jnp.exp(sc - m)
        l = jnp.sum(p, axis=-1, keepdims=True)
        outs.append(_dot(p.astype(BF16), kv_ref[:, inner + hh * hd:inner + (hh + 1) * hd]) / l)
    o = jnp.concatenate(outs, axis=1).astype(BF16)
    y = _layer_norm(alpha * x_ref[...] + _dot(o, wo_ref[...]), g_ref[...], b_ref[...])
    o_ref[...] = y
    pk_ref[...] = _pack_halves(y)
    yb = y.astype(BF16)
    yl = (y - yb.astype(F32)).astype(BF16)
    rw = rw_ref[...]
    rwh = rw.astype(BF16)
    rwl = (rw - rwh.astype(F32)).astype(BF16)
    lg_ref[...] = _dot(yb, rwh) + _dot(yl, rwh) + _dot(yb, rwl) + rb_ref[...]


def mem_attention_ln(xb, x, kv, wq, wo, g, b, rw, rb, segs, *, alpha, tm, n_mem):
    m, d = x.shape
    inner = wq.shape[1]
    full = lambda a: pl.BlockSpec(a.shape, lambda i: (0, 0))
    tile = pl.BlockSpec((tm, d), lambda i: (i, 0))
    kern = functools.partial(_mem_kernel, alpha=alpha)
    return pl.pallas_call(
        kern,
        grid=(m // tm,),
        in_specs=[tile, tile,
                  pl.BlockSpec((n_mem, 2 * inner), lambda i: (segs.batch_of_tile(i, tm), 0)),
                  full(wq), full(wo), full(g), full(b), full(rw), full(rb)],
        out_specs=[tile, pl.BlockSpec((tm, d // 2), lambda i: (i, 0)),
                   pl.BlockSpec((tm, LANES), lambda i: (i, 0))],
        out_shape=[jax.ShapeDtypeStruct((m, d), F32), jax.ShapeDtypeStruct((m, d // 2), U32),
                   jax.ShapeDtypeStruct((m, LANES), F32)],
        compiler_params=_cparams(("arbitrary",)),
        name="mem_attn_ln",
    )(xb, x, kv, wq, wo, g, b, rw, rb)


def _route_kernel(lg_ref, pos_ref, gate_ref, meta_ref, cnt_ref, run_ref, pst_ref,
                  *, n_experts, tm, tt):
    ph = pl.program_id(0)
    i = pl.program_id(1)
    lane = lax.broadcasted_iota(jnp.int32, (tt, LANES), 1)
    lanef = lane.astype(F32)
    work = jnp.where(lane < n_experts, lg_ref[...], -jnp.inf)
    vals, hots = [], []
    for _ in range(TOP_K):
        m = jnp.max(work, axis=-1, keepdims=True)
        idx = jnp.min(jnp.where(work == m, lanef, float(LANES)), axis=-1, keepdims=True)
        hot = lanef == idx
        vals.append(m)
        hots.append(hot)
        work = jnp.where(hot, -jnp.inf, work)
    sel = functools.reduce(jnp.logical_or, hots).astype(F32)
    colsum = jnp.sum(sel, axis=0, keepdims=True)

    @pl.when((ph == 0) & (i == 0))
    def _():
        cnt_ref[...] = jnp.zeros_like(cnt_ref)

    @pl.when(ph == 0)
    def _():
        cnt_ref[...] += colsum

    @pl.when((ph == 1) & (i == 0))
    def _():
        cnt = cnt_ref[...]
        nblk = jnp.floor((cnt + (tm - 1)) / tm)
        a = lax.broadcasted_iota(jnp.int32, (LANES, LANES), 0)
        b = lax.broadcasted_iota(jnp.int32, (LANES, LANES), 1)
        upper = (a < b).astype(BF16)
        first = _dot(jnp.broadcast_to(nblk, (8, LANES)).astype(BF16), upper)[0:1]
        pst_ref[...] = first * tm
        run_ref[...] = jnp.zeros_like(run_ref)
        row = lax.broadcasted_iota(jnp.int32, meta_ref.shape, 0)
        meta = jnp.where(row == 0, cnt, jnp.where(row == 1, first * tm, jnp.where(row == 2, first + nblk, 0.0)))
        meta_ref[...] = meta.astype(jnp.int32)

    @pl.when(ph == 1)
    def _():
        r = lax.broadcasted_iota(jnp.int32, (tt, tt), 0)
        c = lax.broadcasted_iota(jnp.int32, (tt, tt), 1)
        before = _dot((r > c).astype(BF16), sel.astype(BF16)) + run_ref[...]
        posfull = pst_ref[...] + before
        e = [jnp.exp(v - vals[0]) for v in vals]
        den = functools.reduce(lambda x, y: x + y, e)
        pos = jnp.zeros((tt, LANES), F32)
        gat = jnp.zeros((tt, LANES), F32)
        for k in range(TOP_K):
            pk = jnp.sum(jnp.where(hots[k], posfull, 0.0), axis=-1, keepdims=True)
            pos = jnp.where(lane == k, pk, pos)
            gat = jnp.where(lane == k, e[k] / den, gat)
        pos_ref[...] = pos.astype(jnp.int32)
        gate_ref[...] = gat
        run_ref[...] += colsum


def route(logits, n_experts, *, tm, tt):
    t = logits.shape[0]
    tile = lambda: pl.BlockSpec((tt, LANES), lambda p, i: (i * p, 0))
    kern = functools.partial(_route_kernel, n_experts=n_experts, tm=tm, tt=tt)
    return pl.pallas_call(
        kern,
        grid=(2, t // tt),
        in_specs=[pl.BlockSpec((tt, LANES), lambda p, i: (i, 0))],
        out_specs=[tile(), tile(), pl.BlockSpec((8, LANES), lambda p, i: (0, 0))],
        out_shape=[jax.ShapeDtypeStruct((t, LANES), jnp.int32), jax.ShapeDtypeStruct((t, LANES), F32),
                   jax.ShapeDtypeStruct((8, LANES), jnp.int32)],
        scratch_shapes=[pltpu.VMEM((1, LANES), F32), pltpu.VMEM((1, LANES), F32), pltpu.VMEM((1, LANES), F32)],
        compiler_params=_cparams(("arbitrary", "arbitrary")),
        name="moe_route",
    )(logits)


def _row_copy(src, si, dst, di, sem):
    return pltpu.make_async_copy(src.at[pl.ds(si, 1), :], dst.at[pl.ds(di, 1), :], sem)


def _dispatch_kernel(cnt_ref, pst_ref, nu_ref, pos_ref, x_ref, xs_ref, zero_scr, sem, *, n_experts, tm, tt):
    i = pl.program_id(0)
    p_rows = xs_ref.shape[0]

    @pl.when(i == 0)
    def _():
        zero_scr[...] = jnp.zeros_like(zero_scr)
        zrows = zero_scr.shape[0]

        def pad_copy(e):
            start = jnp.minimum((pst_ref[e] + cnt_ref[e]) // SUBLANES * SUBLANES, p_rows - zrows)
            return pltpu.make_async_copy(zero_scr, xs_ref.at[pl.ds(pl.multiple_of(start, SUBLANES), zrows), :], sem)

        def start_pad(e, c):
            pad_copy(e).start()
            return c

        def wait_pad(e, c):
            pad_copy(e).wait()
            return c

        lax.fori_loop(0, n_experts, start_pad, 0)
        lax.fori_loop(0, n_experts, wait_pad, 0)

        def zero_tail(blk, c):
            cp = pltpu.make_async_copy(zero_scr.at[pl.ds(0, tm), :],
                                       xs_ref.at[pl.ds(pl.multiple_of(blk * tm, tm), tm), :], sem)
            cp.start()
            cp.wait()
            return c

        lax.fori_loop(nu_ref[0], p_rows // tm, zero_tail, 0)

    def start_rows(r, c):
        for k in range(TOP_K):
            _row_copy(x_ref, r, xs_ref, pos_ref[r * TOP_K + k], sem).start()
        return c

    def wait_rows(r, c):
        for k in range(TOP_K):
            _row_copy(x_ref, 0, xs_ref, 0, sem).wait()
        return c

    lax.fori_loop(0, tt, start_rows, 0)
    lax.fori_loop(0, tt, wait_rows, 0)


def moe_dispatch(xpk, pos_flat, cnt, pst, n_used, p_rows, *, n_experts, tm, tt):
    t, dh = xpk.shape
    kern = functools.partial(_dispatch_kernel, n_experts=n_experts, tm=tm, tt=tt)
    return pl.pallas_call(
        kern,
        grid_spec=pltpu.PrefetchScalarGridSpec(
            num_scalar_prefetch=3,
            grid=(t // tt,),
            in_specs=[pl.BlockSpec((tt * TOP_K,), lambda i, *_: (i,), memory_space=pltpu.SMEM),
                      pl.BlockSpec((tt, dh), lambda i, *_: (i, 0))],
            out_specs=pl.BlockSpec(memory_space=pl.ANY),
            scratch_shapes=[pltpu.VMEM((tm + SUBLANES, dh), U32), pltpu.SemaphoreType.DMA(())],
        ),
        out_shape=jax.ShapeDtypeStruct((p_rows, dh), U32),
        compiler_params=_cparams(("arbitrary",)),
        name="moe_dispatch",
    )(cnt, pst, n_used, pos_flat, xpk)


def _new_expert(be_ref, i):
    prev = be_ref[jnp.maximum(i - 1, 0)]
    return (i == 0) | (be_ref[i] != prev)


def _moe_gu_kernel(be_ref, nu_ref, x_ref, wg_ref, wu_ref, bg_ref, bu_ref, a_ref, wgb, wub):
    i = pl.program_id(1)
    hw = x_ref.shape[1]

    @pl.when(_new_expert(be_ref, i))
    def _():
        wgb[...] = wg_ref[...].astype(BF16)
        wub[...] = wu_ref[...].astype(BF16)

    @pl.when(i < nu_ref[0])
    def _():
        lo, hi = _unpack_halves(x_ref[...])
        gate = _dot(lo, wgb[:hw, :]) + _dot(hi, wgb[hw:, :]) + bg_ref[...]
        up = _dot(lo, wub[:hw, :]) + _dot(hi, wub[hw:, :]) + bu_ref[...]
        gate = jnp.minimum(gate, SWIGLU_LIMIT)
        up = jnp.clip(up, -SWIGLU_LIMIT, SWIGLU_LIMIT)
        sig = 1.0 / (1.0 + jnp.exp(-SWIGLU_ALPHA * gate))
        a_ref[...] = (gate * sig * (up + 1.0)).astype(a_ref.dtype)

    @pl.when(i >= nu_ref[0])
    def _():
        a_ref[...] = jnp.zeros_like(a_ref)


def _used_block(i, nu):
    return jnp.where(i < nu[0], i, 0)


def moe_gate_up(xs, w_gu, b_gu, layer, blk_e, n_used, *, tm, tn):
    p, hw = xs.shape
    d = 2 * hw
    f = w_gu.shape[3] // 2
    nj = f // tn
    wspec = lambda off: pl.BlockSpec((None, None, d, tn), lambda j, i, be, nu: (layer, be[i], 0, off + j))
    bspec = lambda off: pl.BlockSpec((None, None, 1, tn), lambda j, i, be, nu: (layer, be[i], 0, off + j))
    return pl.pallas_call(
        _moe_gu_kernel,
        grid_spec=pltpu.PrefetchScalarGridSpec(
            num_scalar_prefetch=2,
            grid=(nj, p // tm),
            in_specs=[pl.BlockSpec((tm, hw), lambda j, i, be, nu: (_used_block(i, nu), 0)),
                      wspec(0), wspec(nj), bspec(0), bspec(nj)],
            out_specs=pl.BlockSpec((tm, tn), lambda j, i, be, nu: (i, j)),
            scratch_shapes=[pltpu.VMEM((d, tn), BF16), pltpu.VMEM((d, tn), BF16)],
        ),
        out_shape=jax.ShapeDtypeStruct((p, f), BF16),
        compiler_params=_cparams(("arbitrary", "arbitrary")),
        name="moe_gate_up",
    )(blk_e, n_used, xs, w_gu, w_gu, b_gu, b_gu)


def _moe_down_kernel(be_ref, nu_ref, a_ref, w_ref, b_ref, y_ref, wb):
    i = pl.program_id(1)

    @pl.when(_new_expert(be_ref, i))
    def _():
        wb[...] = w_ref[...].astype(BF16)

    @pl.when(i < nu_ref[0])
    def _():
        y_ref[...] = _dot(a_ref[...], wb[...]) + b_ref[...]

    @pl.when(i >= nu_ref[0])
    def _():
        y_ref[...] = jnp.zeros_like(y_ref)


def moe_down(a, w_down, b_down, layer, blk_e, n_used, *, tm, tn):
    p, f = a.shape
    d = w_down.shape[3]
    return pl.pallas_call(
        _moe_down_kernel,
        grid_spec=pltpu.PrefetchScalarGridSpec(
            num_scalar_prefetch=2,
            grid=(d // tn, p // tm),
            in_specs=[pl.BlockSpec((tm, f), lambda j, i, be, nu: (i, 0)),
                      pl.BlockSpec((None, None, f, tn), lambda j, i, be, nu: (layer, be[i], 0, j)),
                      pl.BlockSpec((None, None, 1, tn), lambda j, i, be, nu: (layer, be[i], 0, j))],
            out_specs=pl.BlockSpec((tm, tn), lambda j, i, be, nu: (i, j)),
            scratch_shapes=[pltpu.VMEM((f, tn), BF16)],
        ),
        out_shape=jax.ShapeDtypeStruct((p, d), F32),
        compiler_params=_cparams(("arbitrary", "arbitrary")),
        name="moe_down",
    )(blk_e, n_used, a, w_down, b_down)


def _combine_kernel(pos_ref, posn_ref, gate_ref, x_ref, g_ref, b_ref, ys_ref, o_ref, ob_ref, buf, sem,
                    *, alpha, tt):
    i = pl.program_id(0)
    n = pl.num_programs(0)
    slot = i % 2

    def start_tile(p_ref, s):
        def body(r, c):
            for k in range(TOP_K):
                _row_copy(ys_ref, p_ref[r * TOP_K + k], buf.at[s, k], r, sem.at[s]).start()
            return c
        lax.fori_loop(0, tt, body, 0)

    @pl.when(i == 0)
    def _():
        start_tile(pos_ref, 0)

    @pl.when(i + 1 < n)
    def _():
        start_tile(posn_ref, 1 - slot)

    def wait_rows(r, c):
        for k in range(TOP_K):
            _row_copy(ys_ref, 0, buf.at[slot, k], 0, sem.at[slot]).wait()
        return c

    lax.fori_loop(0, tt, wait_rows, 0)
    gates = gate_ref[...]
    f = gates[:, 0:1] * buf[slot, 0]
    for k in range(1, TOP_K):
        f = f + gates[:, k:k + 1] * buf[slot, k]
    y = _layer_norm(alpha * x_ref[...] + f, g_ref[...], b_ref[...])
    o_ref[...] = y
    ob_ref[...] = y.astype(BF16)


def moe_combine_ln(ys, pos_flat, gates, x, g, b, *, alpha, tt):
    t, d = x.shape
    nt = t // tt
    tile = pl.BlockSpec((tt, d), lambda i: (i, 0))
    row = pl.BlockSpec((1, d), lambda i: (0, 0))
    kern = functools.partial(_combine_kernel, alpha=alpha, tt=tt)
    return pl.pallas_call(
        kern,
        grid=(nt,),
        in_specs=[pl.BlockSpec((tt * TOP_K,), lambda i: (i,), memory_space=pltpu.SMEM),
                  pl.BlockSpec((tt * TOP_K,), lambda i: (jnp.minimum(i + 1, nt - 1),), memory_space=pltpu.SMEM),
                  pl.BlockSpec((tt, LANES), lambda i: (i, 0)),
                  tile, row, row,
                  pl.BlockSpec(memory_space=pl.ANY)],
        out_specs=[tile, tile],
        out_shape=[jax.ShapeDtypeStruct((t, d), F32), jax.ShapeDtypeStruct((t, d), BF16)],
        scratch_shapes=[pltpu.VMEM((2, TOP_K, tt, d), F32), pltpu.SemaphoreType.DMA((2,))],
        compiler_params=_cparams(("arbitrary",)),
        name="moe_combine_ln",
    )(pos_flat, pos_flat, gates, x, g, b, ys)


def kernel(x_prompt, x_sample, mem_prompt, mem_sample, ev_w_in, ev_w_out, ev_sink, ev_lam_q1, ev_lam_k1, ev_lam_q2, ev_lam_k2, ev_dnorm_w, od_w_in, od_w_out, od_conv_w, mem_wq, mem_wkv, mem_wo, ln_g, ln_b, moe_router_w, moe_router_b, moe_w_gu, moe_b_gu, moe_w_down, moe_b_down):
    b1, s1, d = x_prompt.shape
    b2, s2, _ = x_sample.shape
    segs = Segs(b1, s1, b2, s2)
    t = segs.t
    depth = ln_g.shape[0]
    n_mem = mem_prompt.shape[1]
    n_experts = moe_router_w.shape[2]
    alpha = (2 * depth) ** 0.25
    n_heads = d // (2 * HEAD_DIM)
    n_kv = n_heads // 4
    c_ch = d // 2
    dils = tuple(dil for _, dil in D_PATTERNS)

    x = jnp.concatenate([x_prompt.reshape(-1, d), x_sample.reshape(-1, d)], axis=0)
    xb = x.astype(BF16)
    memb = jnp.concatenate([mem_prompt.reshape(-1, d), mem_sample.reshape(-1, d)], axis=0).astype(BF16)
    tables = rope_tables(max(s1, s2))
    ev_modes = jnp.array([MODE_ROPE128_Q] * n_heads + [MODE_ROPE128] * n_kv + [MODE_NONE] * n_kv
                         + [MODE_ROPE64_Q] * n_heads + [MODE_ROPE64] * n_heads
                         + [MODE_NONE] * n_heads, jnp.int32)
    od_modes = jnp.zeros((3 * c_ch // LANES,), jnp.int32)
    mem_modes = jnp.zeros((mem_wkv.shape[2] // LANES,), jnp.int32)
    mem_segs = Segs(b1, n_mem, b2, n_mem)
    rw_pad = jnp.pad(moe_router_w, ((0, 0), (0, 0), (0, LANES - n_experts)))
    rb_pad = jnp.pad(moe_router_b, ((0, 0), (0, LANES - n_experts)))[:, None, :]
    b_gu4 = moe_b_gu[:, :, None, :]
    b_down4 = moe_b_down[:, :, None, :]
    moe_tm = 512
    n_blocks = -(-(t * TOP_K + n_experts * (moe_tm - 1)) // moe_tm)
    p_rows = n_blocks * moe_tm

    for l in range(depth):
        j = l // 2
        if l % 2 == 0:
            lambda_init = 0.8 - 0.6 * math.exp(-0.3 * l)
            w_in = ev_w_in[j].astype(BF16)
            n_in = w_in.shape[1]
            tn = n_in // 6 if (n_in // 6) % LANES == 0 else n_in // 18
            h = proj(xb, w_in, ev_modes, tables, segs, col0=0, ncols=n_in, tm=512, tn=tn,
                     out_dtype=BF16)
            lamv = jnp.stack([ev_lam_q1[j], ev_lam_k1[j], ev_lam_q2[j], ev_lam_k2[j]])
            oa, ob = [], []
            for row0, bsz, s in segs.groups():
                oa.append(window_attention(h, ev_sink[j], row0, bsz, s, d_model=d))
                ob.append(diff_attention(h, lamv, ev_dnorm_w[j][None, :], row0, bsz, s, lambda_init,
                                         d_model=d, tq=1024, tk=2048, n_streams=4))
            fs = [jnp.concatenate(oa, axis=0), jnp.concatenate(ob, axis=0)]
            w_out = ev_w_out[j].astype(BF16)
        else:
            w_in = od_w_in[j].astype(BF16)
            hc = proj(xb, w_in, od_modes, tables, segs, col0=0, ncols=3 * c_ch, tm=512,
                      tn=c_ch, out_dtype=F32)
            hqs = [proj_dilated(xb, w_in, tables, segs, g, dil, col0=3 * c_ch, tm=512)
                   for g, dil in enumerate(dils)]
            oc, od = [], []
            for row0, bsz, s in segs.groups():
                oc.append(gated_conv(hc, od_conv_w[j], row0, bsz, s, ts=512, tc=c_ch // 2))
                outs = [dilated_group(hq, dil, row0, bsz, s) for hq, dil in zip(hqs, dils)]
                od.append(dilated_combine([o for o, _ in outs], [ls for _, ls in outs], dils, tm=512))
            fs = [jnp.concatenate(oc, axis=0), jnp.concatenate(od, axis=0)]
            w_out = od_w_out[j].astype(BF16)
        x, xb = outproj_ln(fs, w_out, x, ln_g[l, 0][None, :], ln_b[l, 0][None, :], alpha=alpha, tm=256)

        kv = proj(memb, mem_wkv[l].astype(BF16), mem_modes, tables, mem_segs, col0=0,
                  ncols=mem_wkv.shape[2], tm=n_mem, tn=mem_wkv.shape[2] // 2, out_dtype=BF16)
        x, xpk, logits = mem_attention_ln(xb, x, kv, mem_wq[l].astype(BF16), mem_wo[l].astype(BF16),
                                          ln_g[l, 1][None, :], ln_b[l, 1][None, :], rw_pad[l], rb_pad[l],
                                          segs, alpha=alpha, tm=256, n_mem=n_mem)

        pos, gates, meta = route(logits, n_experts, tm=moe_tm, tt=512)
        pos_flat = pos[:, :TOP_K].reshape(-1)
        cnt, pst, end_blk = meta[0, :n_experts], meta[1, :n_experts], meta[2, :n_experts]
        blk_e = jnp.minimum(jnp.sum(end_blk[None, :] <= jnp.arange(n_blocks, dtype=jnp.int32)[:, None], axis=1),
                            n_experts - 1).astype(jnp.int32)
        n_used = end_blk[n_experts - 1:]
        xs = moe_dispatch(xpk, pos_flat, cnt, pst, n_used, p_rows, n_experts=n_experts, tm=moe_tm, tt=256)
        a = moe_gate_up(xs, moe_w_gu, b_gu4, l, blk_e, n_used, tm=moe_tm, tn=512)
        ys = moe_down(a, moe_w_down, b_down4, l, blk_e, n_used, tm=moe_tm, tn=1024)
        x, xb = moe_combine_ln(ys, pos_flat, gates, x, ln_g[l, 2][None, :], ln_b[l, 2][None, :],
                               alpha=alpha, tt=128)

    y1 = x[:segs.t1].reshape(b1, s1, d)
    y2 = x[segs.t1:].reshape(b2, s2, d)
    return (y1, y2)
```

```python
import functools
import math

import jax
import jax.numpy as jnp
from jax import lax
from jax.experimental import pallas as pl
from jax.experimental.pallas import tpu as pltpu

F32 = jnp.float32
BF16 = jnp.bfloat16
U32 = jnp.uint32
LANES = 128
SUBLANES = 8
VMEM_LIMIT = 56 * 1024 * 1024

HEAD_DIM = 128
ROPE_THETA = 10000.0
A_WINDOW = 128
B_DIM = 64
C_WIDTH = 3
D_PATTERNS = ((128, 1), (512, 4), (2048, 16))
D_HEADS_PER_GROUP = 4
D_NK = 64
MEM_HEADS = 4
TOP_K = 4
SWIGLU_LIMIT = 7.0
SWIGLU_ALPHA = 1.702
LN_EPS = 1e-5
NEG_INF = -1e30

MODE_NONE, MODE_ROPE128, MODE_ROPE128_Q, MODE_ROPE64, MODE_ROPE64_Q = 0, 1, 2, 3, 4


def _cparams(sem):
    return pltpu.CompilerParams(dimension_semantics=sem, vmem_limit_bytes=VMEM_LIMIT)


def _dot(a, b):
    return jnp.dot(a, b, preferred_element_type=F32)


def _dot_t(a, b):
    return lax.dot_general(a, b, (((1,), (1,)), ((), ())), preferred_element_type=F32)


class Segs:
    def __init__(self, b1, s1, b2, s2):
        self.b1, self.s1, self.b2, self.s2 = b1, s1, b2, s2
        self.t1 = b1 * s1
        self.t = b1 * s1 + b2 * s2

    def groups(self):
        return ((0, self.b1, self.s1), (self.t1, self.b2, self.s2))

    def pos_tile(self, i, tm):
        n1 = self.t1 // tm
        return jnp.where(i < n1, i % (self.s1 // tm), (i - n1) % (self.s2 // tm))

    def batch_of_tile(self, i, tm):
        n1 = self.t1 // tm
        return jnp.where(i < n1, (i * tm) // self.s1, self.b1 + ((i - n1) * tm) // self.s2)


def _layer_norm(z, g, b):
    mu = jnp.mean(z, axis=-1, keepdims=True)
    zc = z - mu
    var = jnp.mean(zc * zc, axis=-1, keepdims=True)
    return zc * lax.rsqrt(var + LN_EPS) * g + b


def _rope128(a, c, s):
    return a * c + pltpu.roll(a, 64, 1) * s


def _proj_kernel(mode_ref, x_ref, w_ref, c128_ref, s128_ref, c64_ref, sa64_ref, sb64_ref, o_ref,
                 *, n_slabs, slab0, q128_scale, q64_scale):
    j = pl.program_id(0)
    acc = _dot(x_ref[...], w_ref[...])
    for s in range(n_slabs):
        a = acc[:, s * LANES:(s + 1) * LANES]
        mode = mode_ref[slab0 + j * n_slabs + s]
        cols = slice(s * LANES, (s + 1) * LANES)

        @pl.when(mode == MODE_NONE)
        def _():
            o_ref[:, cols] = a.astype(o_ref.dtype)

        def rope128(scale):
            r = _rope128(a, c128_ref[...], s128_ref[...])
            o_ref[:, cols] = (r * scale).astype(o_ref.dtype)

        def rope64(scale):
            r = (a * c64_ref[...] + pltpu.roll(a, 96, 1) * sa64_ref[...]
                 + pltpu.roll(a, 32, 1) * sb64_ref[...])
            o_ref[:, cols] = (r * scale).astype(o_ref.dtype)

        pl.when(mode == MODE_ROPE128)(functools.partial(rope128, 1.0))
        pl.when(mode == MODE_ROPE128_Q)(functools.partial(rope128, q128_scale))
        pl.when(mode == MODE_ROPE64)(functools.partial(rope64, 1.0))
        pl.when(mode == MODE_ROPE64_Q)(functools.partial(rope64, q64_scale))


def proj(xb, w, modes, tables, segs, *, col0, ncols, tm, tn, out_dtype):
    m, k = xb.shape
    assert ncols % tn == 0 and col0 % tn == 0 and m % tm == 0 and tn % LANES == 0
    n_slabs = tn // LANES
    jb = col0 // tn
    tab_spec = pl.BlockSpec((tm, LANES), lambda j, i, *_: (segs.pos_tile(i, tm), 0))
    kern = functools.partial(_proj_kernel, n_slabs=n_slabs, slab0=col0 // LANES,
                             q128_scale=HEAD_DIM ** -0.5, q64_scale=B_DIM ** -0.5)
    return pl.pallas_call(
        kern,
        grid_spec=pltpu.PrefetchScalarGridSpec(
            num_scalar_prefetch=1,
            grid=(ncols // tn, m // tm),
            in_specs=[pl.BlockSpec((tm, k), lambda j, i, *_: (i, 0)),
                      pl.BlockSpec((k, tn), lambda j, i, *_: (0, jb + j)),
                      tab_spec, tab_spec, tab_spec, tab_spec, tab_spec],
            out_specs=pl.BlockSpec((tm, tn), lambda j, i, *_: (i, j)),
        ),
        out_shape=jax.ShapeDtypeStruct((m, ncols), out_dtype),
        compiler_params=_cparams(("arbitrary", "arbitrary")),
        name="proj",
    )(modes, xb, w, *tables)


def rope_tables(s):
    inv = ROPE_THETA ** (-jnp.arange(0, HEAD_DIM, 2, dtype=F32) / HEAD_DIM)
    ang = jnp.arange(s, dtype=F32)[:, None] * inv[None, :]
    c, sn = jnp.cos(ang), jnp.sin(ang)
    c128 = jnp.concatenate([c, c], axis=1)
    s128 = jnp.concatenate([-sn, sn], axis=1)
    invb = ROPE_THETA ** (-jnp.arange(0, B_DIM, 2, dtype=F32) / B_DIM)
    angb = jnp.arange(s, dtype=F32)[:, None] * invb[None, :]
    cb, sb = jnp.cos(angb), jnp.sin(angb)
    z = jnp.zeros_like(sb)
    c64 = jnp.concatenate([cb, cb, cb, cb], axis=1)
    sa64 = jnp.concatenate([-sb, z, -sb, z], axis=1)
    sb64 = jnp.concatenate([z, sb, z, sb], axis=1)
    return (c128, s128, c64, sa64, sb64)


def _window_kernel(sink_ref, q_ref, kp_ref, kc_ref, kn_ref, vp_ref, vc_ref, vn_ref, o_ref,
                   *, n_heads, n_kv, blk):
    n = pl.program_id(1)
    nb = pl.num_programs(1)
    r = lax.broadcasted_iota(jnp.int32, (blk, 3 * blk), 0)
    c = lax.broadcasted_iota(jnp.int32, (blk, 3 * blk), 1)
    valid = jnp.abs(r + blk - c) <= A_WINDOW
    valid &= (c >= blk) | (n > 0)
    valid &= (c < 2 * blk) | (n < nb - 1)
    g = n_heads // n_kv
    for kv in range(n_kv):
        cs = slice(kv * HEAD_DIM, (kv + 1) * HEAD_DIM)
        k3 = jnp.concatenate([kp_ref[:, cs], kc_ref[:, cs], kn_ref[:, cs]], axis=0)
        v3 = jnp.concatenate([vp_ref[:, cs], vc_ref[:, cs], vn_ref[:, cs]], axis=0)
        for gi in range(g):
            h = kv * g + gi
            hs = slice(h * HEAD_DIM, (h + 1) * HEAD_DIM)
            sc = jnp.where(valid, _dot_t(q_ref[:, hs], k3), NEG_INF)
            sk = sink_ref[h]
            m = jnp.maximum(jnp.max(sc, axis=-1, keepdims=True), sk)
            p = jnp.exp(sc - m)
            denom = jnp.sum(p, axis=-1, keepdims=True) + jnp.exp(sk - m)
            o = _dot(p.astype(BF16), v3) / denom
            o_ref[:, hs] = o.astype(o_ref.dtype)


def window_attention(h, sink, row0, bsz, s, *, d_model):
    blk = A_WINDOW
    n_heads = d_model // (2 * HEAD_DIM)
    n_kv = n_heads // 4
    qw, kw = n_heads * HEAD_DIM, n_kv * HEAD_DIM
    nb = s // blk
    rb0 = row0 // blk
    kcol, vcol = qw // kw, qw // kw + 1

    def rows(b, n):
        return rb0 + b * nb + n

    def spec(col, shift):
        def imap(b, n):
            nn = jnp.clip(n + shift, 0, nb - 1)
            return (rows(b, nn), col)
        return pl.BlockSpec((blk, kw), imap)

    kern = functools.partial(_window_kernel, n_heads=n_heads, n_kv=n_kv, blk=blk)
    return pl.pallas_call(
        kern,
        grid=(bsz, nb),
        in_specs=[pl.BlockSpec(memory_space=pltpu.SMEM),
                  pl.BlockSpec((blk, qw), lambda b, n: (rows(b, n), 0)),
                  spec(kcol, -1), spec(kcol, 0), spec(kcol, 1),
                  spec(vcol, -1), spec(vcol, 0), spec(vcol, 1)],
        out_specs=pl.BlockSpec((blk, qw), lambda b, n: (b * nb + n, 0)),
        out_shape=jax.ShapeDtypeStruct((bsz * s, qw), BF16),
        compiler_params=_cparams(("arbitrary", "arbitrary")),
        name="window_attn",
    )(sink, h, h, h, h, h, h, h)


def _diff_kernel(lamv_ref, nw_ref, q_ref, k_ref, v_ref, o_ref, *, tq, tk, n_streams, lambda_init):
    s = k_ref.shape[0]
    ts = tq // n_streams
    lane = lax.broadcasted_iota(jnp.int32, (ts, LANES), 1)
    q2s = []
    for st in range(n_streams):
        q = q_ref[st * ts:(st + 1) * ts, :]
        zero = jnp.zeros_like(q)
        q2s.append(jnp.concatenate([jnp.where(lane < B_DIM, q, zero),
                                    jnp.where(lane >= B_DIM, q, zero)], axis=0))

    def body(i, carry):
        ks = pl.multiple_of(i * tk, tk)
        kb = k_ref[pl.ds(ks, tk), :]
        vb = v_ref[pl.ds(ks, tk), :]
        out = []
        for st in range(n_streams):
            m, l, acc = carry[st]
            sc = _dot_t(q2s[st], kb)
            m_new = jnp.maximum(m, jnp.max(sc, axis=-1, keepdims=True))
            a = jnp.exp(m - m_new)
            p = jnp.exp(sc - m_new)
            l = a * l + jnp.sum(p, axis=-1, keepdims=True)
            acc = a * acc + _dot(p.astype(BF16), vb)
            out.append((m_new, l, acc))
        return tuple(out)

    init = tuple((jnp.full((2 * ts, 1), NEG_INF, F32), jnp.zeros((2 * ts, 1), F32),
                  jnp.zeros((2 * ts, LANES), F32)) for _ in range(n_streams))
    res = lax.fori_loop(0, s // tk, body, init)
    lv = lamv_ref[...]
    lam = (jnp.exp(jnp.sum(lv[0:1] * lv[1:2], axis=-1, keepdims=True))
           - jnp.exp(jnp.sum(lv[2:3] * lv[3:4], axis=-1, keepdims=True)) + lambda_init)
    for st in range(n_streams):
        m, l, acc = res[st]
        on = acc / l
        of = on[:ts] - lam * on[ts:]
        of = of * lax.rsqrt(jnp.mean(of * of, axis=-1, keepdims=True) + LN_EPS) * nw_ref[...]
        o_ref[st * ts:(st + 1) * ts, :] = (of * (1.0 - lambda_init)).astype(o_ref.dtype)


def diff_attention(h, lamv, norm_w, row0, bsz, s, lambda_init, *, d_model, tq, tk, n_streams):
    n_heads = d_model // (2 * HEAD_DIM)
    a_cols = (n_heads + 2 * (n_heads // 4))
    qc, kc, vc = a_cols, a_cols + n_heads, a_cols + 2 * n_heads
    nq = s // tq
    assert row0 % s == 0
    sb0 = row0 // s
    qb0 = row0 // tq
    kern = functools.partial(_diff_kernel, tq=tq, tk=tk, n_streams=n_streams, lambda_init=lambda_init)
    return pl.pallas_call(
        kern,
        grid=(bsz, n_heads, nq),
        in_specs=[pl.BlockSpec((4, B_DIM), lambda b, hh, n: (0, 0)),
                  pl.BlockSpec((1, LANES), lambda b, hh, n: (0, 0)),
                  pl.BlockSpec((tq, LANES), lambda b, hh, n: (qb0 + b * nq + n, qc + hh)),
                  pl.BlockSpec((s, LANES), lambda b, hh, n: (sb0 + b, kc + hh)),
                  pl.BlockSpec((s, LANES), lambda b, hh, n: (sb0 + b, vc + hh))],
        out_specs=pl.BlockSpec((tq, LANES), lambda b, hh, n: (b * nq + n, hh)),
        out_shape=jax.ShapeDtypeStruct((bsz * s, n_heads * LANES), BF16),
        compiler_params=_cparams(("arbitrary", "arbitrary", "arbitrary")),
        name="diff_attn",
    )(lamv, norm_w, h, h, h)


def _conv_kernel(cw_ref, b_ref, c_ref, x_ref, cp_ref, xp_ref, cn_ref, xn_ref, o_ref, *, ts):
    n = pl.program_id(1)
    nt = pl.num_programs(1)
    u = c_ref[...] * x_ref[...]
    prev = jnp.where(n > 0, cp_ref[7:8, :] * xp_ref[7:8, :], 0.0)
    nxt = jnp.where(n < nt - 1, cn_ref[0:1, :] * xn_ref[0:1, :], 0.0)
    row = lax.broadcasted_iota(jnp.int32, u.shape, 0)
    um1 = jnp.where(row == 0, prev, pltpu.roll(u, 1, 0))
    up1 = jnp.where(row == ts - 1, nxt, pltpu.roll(u, ts - 1, 0))
    conv = um1 * cw_ref[0:1, :] + u * cw_ref[1:2, :] + up1 * cw_ref[2:3, :]
    o_ref[...] = (b_ref[...] * conv).astype(o_ref.dtype)


def gated_conv(hc, conv_w, row0, bsz, s, *, ts, tc):
    c = hc.shape[1] // 3
    nt = s // ts
    nc = c // tc
    rb0 = row0 // ts
    r8 = ts // 8

    def main(col):
        return pl.BlockSpec((ts, tc), lambda b, n, j: (rb0 + b * nt + n, col * nc + j))

    def halo(col, shift):
        def imap(b, n, j):
            blk8 = (rb0 + b * nt + n) * r8 + (-1 if shift < 0 else r8)
            return (jnp.clip(blk8, 0, hc.shape[0] // 8 - 1), col * nc + j)
        return pl.BlockSpec((8, tc), imap)

    kern = functools.partial(_conv_kernel, ts=ts)
    return pl.pallas_call(
        kern,
        grid=(bsz, nt, nc),
        in_specs=[pl.BlockSpec((C_WIDTH, tc), lambda b, n, j: (0, j)),
                  main(0), main(1), main(2), halo(1, -1), halo(2, -1), halo(1, 1), halo(2, 1)],
        out_specs=pl.BlockSpec((ts, tc), lambda b, n, j: (b * nt + n, j)),
        out_shape=jax.ShapeDtypeStruct((bsz * s, c), BF16),
        compiler_params=_cparams(("arbitrary", "arbitrary", "arbitrary")),
        name="gated_conv",
    )(conv_w, hc, hc, hc, hc, hc, hc, hc)


def _proj_dil_kernel(x_ref, w_ref, c_ref, s_ref, o_ref, *scrs, dil, tm, q_scale):
    sec = pl.program_id(0)
    rows = tm // dil
    acc = _dot(x_ref[...], w_ref[...])
    n_slabs = len(scrs)
    for sl, scr in enumerate(scrs):
        scr[...] = acc[:, sl * LANES:(sl + 1) * LANES]

    def emit(rope, scale):
        for r in range(dil):
            rs = pl.ds(r, rows, stride=dil) if dil > 1 else pl.ds(0, rows)
            for sl in range(n_slabs):
                cols = slice(sl * LANES, (sl + 1) * LANES)
                a = scrs[sl][rs, :]
                if rope:
                    a = _rope128(a, c_ref[rs, :], s_ref[rs, :]) * scale
                o_ref[r, :, cols] = a.astype(o_ref.dtype)

    pl.when(sec == 0)(functools.partial(emit, True, q_scale))
    pl.when(sec == 1)(functools.partial(emit, True, 1.0))
    pl.when(sec == 2)(functools.partial(emit, False, 1.0))


def proj_dilated(xb, w, tables, segs, g, dil, *, col0, tm):
    m, k = xb.shape
    gw = D_HEADS_PER_GROUP * HEAD_DIM
    n_groups = len(D_PATTERNS)
    cb0 = col0 // gw
    tab = pl.BlockSpec((tm, LANES), lambda j, i: (segs.pos_tile(i, tm), 0))
    kern = functools.partial(_proj_dil_kernel, dil=dil, tm=tm, q_scale=HEAD_DIM ** -0.5)
    return pl.pallas_call(
        kern,
        grid=(3, m // tm),
        in_specs=[pl.BlockSpec((tm, k), lambda j, i: (i, 0)),
                  pl.BlockSpec((k, gw), lambda j, i: (0, cb0 + j * n_groups + g)),
                  tab, tab],
        out_specs=pl.BlockSpec((dil, tm // dil, gw), lambda j, i: (0, i, j)),
        out_shape=jax.ShapeDtypeStruct((dil, m // dil, 3 * gw), BF16),
        scratch_shapes=[pltpu.VMEM((tm, LANES), F32) for _ in range(gw // LANES)],
        compiler_params=_cparams(("arbitrary", "arbitrary")),
        name="proj_dilated",
    )(xb, w, tables[0], tables[1])


def _dilated_kernel(q_ref, kp_ref, kc_ref, kn_ref, vp_ref, vc_ref, vn_ref, o_ref, lse_ref,
                    *, blk, half):
    n = pl.program_id(2)
    nt = pl.num_programs(2)
    wk = blk + 2 * half
    r = lax.broadcasted_iota(jnp.int32, (blk, wk), 0)
    c = lax.broadcasted_iota(jnp.int32, (blk, wk), 1)
    valid = (c >= r) & (c <= r + 2 * half)
    valid &= (c >= half) | (n > 0)
    valid &= (c < half + blk) | (n < nt - 1)
    lane = lax.broadcasted_iota(jnp.int32, (blk, LANES), 1)
    lse_all = jnp.zeros((blk, LANES), F32)
    per = LANES // D_HEADS_PER_GROUP
    for hh in range(D_HEADS_PER_GROUP):
        hs = slice(hh * HEAD_DIM, (hh + 1) * HEAD_DIM)
        kk = jnp.concatenate([kp_ref[:, hs], kc_ref[:, hs], kn_ref[:, hs]], axis=0)
        vv = jnp.concatenate([vp_ref[:, hs], vc_ref[:, hs], vn_ref[:, hs]], axis=0)
        sc = jnp.where(valid, _dot_t(q_ref[:, hs], kk), NEG_INF)
        m = jnp.max(sc, axis=-1, keepdims=True)
        p = jnp.exp(sc - m)
        l = jnp.sum(p, axis=-1, keepdims=True)
        o_ref[:, hs] = _dot(p.astype(BF16), vv) / l
        lse_all = jnp.where(lane // per == hh, m + jnp.log(l), lse_all)
    lse_ref[...] = lse_all


def dilated_group(hq, dil, row0, bsz, s):
    blk, half = 2 * D_NK, D_NK
    gw = D_HEADS_PER_GROUP * HEAD_DIM
    sd = s // dil
    nt = sd // blk
    assert (row0 // dil) % blk == 0 and sd % blk == 0
    rb0 = row0 // dil // blk
    hb = blk // half

    def cur(sec):
        return pl.BlockSpec((None, blk, gw), lambda b, r, n: (r, rb0 + b * nt + n, sec))

    def halo(sec, shift):
        def imap(b, r, n):
            nn = jnp.clip(n * hb + (-1 if shift < 0 else hb), 0, nt * hb - 1)
            return (r, (rb0 + b * nt) * hb + nn, sec)
        return pl.BlockSpec((None, half, gw), imap)

    kern = functools.partial(_dilated_kernel, blk=blk, half=half)
    return pl.pallas_call(
        kern,
        grid=(bsz, dil, nt),
        in_specs=[cur(0), halo(1, -1), cur(1), halo(1, 1), halo(2, -1), cur(2), halo(2, 1)],
        out_specs=[pl.BlockSpec((None, blk, gw), lambda b, r, n: (r, b * nt + n, 0)),
                   pl.BlockSpec((None, blk, LANES), lambda b, r, n: (r, b * nt + n, 0))],
        out_shape=[jax.ShapeDtypeStruct((dil, bsz * sd, gw), F32),
                   jax.ShapeDtypeStruct((dil, bsz * sd, LANES), F32)],
        compiler_params=_cparams(("arbitrary", "arbitrary", "arbitrary")),
        name="dilated_attn",
    )(hq, hq, hq, hq, hq, hq, hq)


def _dil_combine_kernel(*refs, dils, tm):
    ng = len(dils)
    nh = D_HEADS_PER_GROUP
    o_refs, l_refs = refs[:ng], refs[ng:2 * ng]
    out_ref = refs[2 * ng]
    scr = refs[2 * ng + 1:]
    o_scr = [scr[g * (nh + 1):g * (nh + 1) + nh] for g in range(ng)]
    l_scr = [scr[g * (nh + 1) + nh] for g in range(ng)]
    for o_ref, l_ref, osc, lsc, dil in zip(o_refs, l_refs, o_scr, l_scr, dils):
        for r in range(dil):
            rs = pl.ds(r, tm // dil, stride=dil) if dil > 1 else pl.ds(0, tm)
            for hh in range(nh):
                osc[hh][rs, :] = o_ref[r, :, hh * HEAD_DIM:(hh + 1) * HEAD_DIM]
            lsc[rs, :] = l_ref[r]
    per = LANES // nh
    for hh in range(nh):
        hs = slice(hh * HEAD_DIM, (hh + 1) * HEAD_DIM)
        ls = slice(hh * per, hh * per + 1)
        lses = [lsc[:, ls] for lsc in l_scr]
        m = functools.reduce(jnp.maximum, lses)
        ws = [jnp.exp(a - m) for a in lses]
        den = functools.reduce(lambda a, b: a + b, ws)
        o = functools.reduce(lambda a, b: a + b, [(w / den) * osc[hh][...] for w, osc in zip(ws, o_scr)])
        out_ref[:, hs] = o.astype(out_ref.dtype)


def dilated_combine(os_, lses, dils, *, tm):
    gw = os_[0].shape[2]
    m = os_[0].shape[0] * os_[0].shape[1]
    ospecs = [pl.BlockSpec((d, tm // d, gw), lambda i: (0, i, 0)) for d in dils]
    lspecs = [pl.BlockSpec((d, tm // d, LANES), lambda i: (0, i, 0)) for d in dils]
    kern = functools.partial(_dil_combine_kernel, dils=dils, tm=tm)
    return pl.pallas_call(
        kern,
        grid=(m // tm,),
        in_specs=ospecs + lspecs,
        out_specs=pl.BlockSpec((tm, gw), lambda i: (i, 0)),
        out_shape=jax.ShapeDtypeStruct((m, gw), BF16),
        scratch_shapes=[pltpu.VMEM((tm, LANES), F32) for _ in range(len(dils) * (D_HEADS_PER_GROUP + 1))],
        compiler_params=_cparams(("arbitrary",)),
        name="dilated_combine",
    )(*os_, *lses)


def _outproj_kernel(*refs, n_in, alpha):
    f_refs = refs[:n_in]
    w_refs = refs[n_in:2 * n_in]
    x_ref, g_ref, b_ref, o_ref, ob_ref = refs[2 * n_in:]
    acc = alpha * x_ref[...]
    for f_ref, w_ref in zip(f_refs, w_refs):
        acc = acc + _dot(f_ref[...], w_ref[...])
    y = _layer_norm(acc, g_ref[...], b_ref[...])
    o_ref[...] = y
    ob_ref[...] = y.astype(BF16)


def outproj_ln(fs, w, x, g, b, *, alpha, tm):
    m, d = x.shape
    n_in = len(fs)
    widths = [f.shape[1] for f in fs]
    in_specs = [pl.BlockSpec((tm, wd), lambda i: (i, 0)) for wd in widths]
    ws = []
    off = 0
    for wd in widths:
        assert off % wd == 0
        in_specs.append(pl.BlockSpec((wd, d), functools.partial(lambda i, o: (o, 0), o=off // wd)))
        ws.append(w)
        off += wd
    row = pl.BlockSpec((1, d), lambda i: (0, 0))
    in_specs += [pl.BlockSpec((tm, d), lambda i: (i, 0)), row, row]
    kern = functools.partial(_outproj_kernel, n_in=n_in, alpha=alpha)
    return pl.pallas_call(
        kern,
        grid=(m // tm,),
        in_specs=in_specs,
        out_specs=[pl.BlockSpec((tm, d), lambda i: (i, 0)), pl.BlockSpec((tm, d), lambda i: (i, 0))],
        out_shape=[jax.ShapeDtypeStruct((m, d), F32), jax.ShapeDtypeStruct((m, d), BF16)],
        compiler_params=_cparams(("arbitrary",)),
        name="outproj_ln",
    )(*fs, *ws, x, g, b)


def _pack_halves(y):
    bits = lax.bitcast_convert_type(y.astype(BF16).astype(F32), U32)
    hw = y.shape[1] // 2
    return (bits[:, hw:] & jnp.uint32(0xFFFF0000)) | (bits[:, :hw] >> 16)


def _unpack_halves(pk):
    lo = lax.bitcast_convert_type(pk << 16, F32).astype(BF16)
    hi = lax.bitcast_convert_type(pk & jnp.uint32(0xFFFF0000), F32).astype(BF16)
    return lo, hi


def _mem_kernel(xb_ref, x_ref, kv_ref, wq_ref, wo_ref, g_ref, b_ref, rw_ref, rb_ref,
                o_ref, pk_ref, lg_ref, *, alpha):
    inner = wq_ref.shape[1]
    hd = inner // MEM_HEADS
    q = (_dot(xb_ref[...], wq_ref[...]) * hd ** -0.5).astype(BF16)
    outs = []
    for hh in range(MEM_HEADS):
        hs = slice(hh * hd, (hh + 1) * hd)
        sc = _dot_t(q[:, hs], kv_ref[:, hs])
        m = jnp.max(sc, axis=-1, keepdims=True)
        p = jnp.exp(sc - m)
        l = jnp.sum(p, axis=-1, keepdims=True)
        outs.append(_dot(p.astype(BF16), kv_ref[:, inner + hh * hd:inner + (hh + 1) * hd]) / l)
    o = jnp.concatenate(outs, axis=1).astype(BF16)
    y = _layer_norm(alpha * x_ref[...] + _dot(o, wo_ref[...]), g_ref[...], b_ref[...])
    o_ref[...] = y
    pk_ref[...] = _pack_halves(y)
    yb = y.astype(BF16)
    yl = (y - yb.astype(F32)).astype(BF16)
    rw = rw_ref[...]
    rwh = rw.astype(BF16)
    rwl = (rw - rwh.astype(F32)).astype(BF16)
    lg_ref[...] = _dot(yb, rwh) + _dot(yl, rwh) + _dot(yb, rwl) + rb_ref[...]


def mem_attention_ln(xb, x, kv, wq, wo, g, b, rw, rb, segs, *, alpha, tm, n_mem):
    m, d = x.shape
    inner = wq.shape[1]
    full = lambda a: pl.BlockSpec(a.shape, lambda i: (0, 0))
    tile = pl.BlockSpec((tm, d), lambda i: (i, 0))
    kern = functools.partial(_mem_kernel, alpha=alpha)
    return pl.pallas_call(
        kern,
        grid=(m // tm,),
        in_specs=[tile, tile,
                  pl.BlockSpec((n_mem, 2 * inner), lambda i: (segs.batch_of_tile(i, tm), 0)),
                  full(wq), full(wo), full(g), full(b), full(rw), full(rb)],
        out_specs=[tile, pl.BlockSpec((tm, d // 2), lambda i: (i, 0)),
                   pl.BlockSpec((tm, LANES), lambda i: (i, 0))],
        out_shape=[jax.ShapeDtypeStruct((m, d), F32), jax.ShapeDtypeStruct((m, d // 2), U32),
                   jax.ShapeDtypeStruct((m, LANES), F32)],
        compiler_params=_cparams(("arbitrary",)),
        name="mem_attn_ln",
    )(xb, x, kv, wq, wo, g, b, rw, rb)


def _route_kernel(lg_ref, pos_ref, gate_ref, meta_ref, cnt_ref, run_ref, pst_ref,
                  *, n_experts, tm, tt):
    ph = pl.program_id(0)
    i = pl.program_id(1)
    lane = lax.broadcasted_iota(jnp.int32, (tt, LANES), 1)
    lanef = lane.astype(F32)
    work = jnp.where(lane < n_experts, lg_ref[...], -jnp.inf)
    vals, hots = [], []
    for _ in range(TOP_K):
        m = jnp.max(work, axis=-1, keepdims=True)
        idx = jnp.min(jnp.where(work == m, lanef, float(LANES)), axis=-1, keepdims=True)
        hot = lanef == idx
        vals.append(m)
        hots.append(hot)
        work = jnp.where(hot, -jnp.inf, work)
    sel = functools.reduce(jnp.logical_or, hots).astype(F32)
    colsum = jnp.sum(sel, axis=0, keepdims=True)

    @pl.when((ph == 0) & (i == 0))
    def _():
        cnt_ref[...] = jnp.zeros_like(cnt_ref)

    @pl.when(ph == 0)
    def _():
        cnt_ref[...] += colsum

    @pl.when((ph == 1) & (i == 0))
    def _():
        cnt = cnt_ref[...]
        nblk = jnp.floor((cnt + (tm - 1)) / tm)
        a = lax.broadcasted_iota(jnp.int32, (LANES, LANES), 0)
        b = lax.broadcasted_iota(jnp.int32, (LANES, LANES), 1)
        upper = (a < b).astype(BF16)
        first = _dot(jnp.broadcast_to(nblk, (8, LANES)).astype(BF16), upper)[0:1]
        pst_ref[...] = first * tm
        run_ref[...] = jnp.zeros_like(run_ref)
        row = lax.broadcasted_iota(jnp.int32, meta_ref.shape, 0)
        meta = jnp.where(row == 0, cnt, jnp.where(row == 1, first * tm, jnp.where(row == 2, first + nblk, 0.0)))
        meta_ref[...] = meta.astype(jnp.int32)

    @pl.when(ph == 1)
    def _():
        r = lax.broadcasted_iota(jnp.int32, (tt, tt), 0)
        c = lax.broadcasted_iota(jnp.int32, (tt, tt), 1)
        before = _dot((r > c).astype(BF16), sel.astype(BF16)) + run_ref[...]
        posfull = pst_ref[...] + before
        e = [jnp.exp(v - vals[0]) for v in vals]
        den = functools.reduce(lambda x, y: x + y, e)
        pos = jnp.zeros((tt, LANES), F32)
        gat = jnp.zeros((tt, LANES), F32)
        for k in range(TOP_K):
            pk = jnp.sum(jnp.where(hots[k], posfull, 0.0), axis=-1, keepdims=True)
            pos = jnp.where(lane == k, pk, pos)
            gat = jnp.where(lane == k, e[k] / den, gat)
        pos_ref[...] = pos.astype(jnp.int32)
        gate_ref[...] = gat
        run_ref[...] += colsum


def route(logits, n_experts, *, tm, tt):
    t = logits.shape[0]
    tile = lambda: pl.BlockSpec((tt, LANES), lambda p, i: (i * p, 0))
    kern = functools.partial(_route_kernel, n_experts=n_experts, tm=tm, tt=tt)
    return pl.pallas_call(
        kern,
        grid=(2, t // tt),
        in_specs=[pl.BlockSpec((tt, LANES), lambda p, i: (i, 0))],
        out_specs=[tile(), tile(), pl.BlockSpec((8, LANES), lambda p, i: (0, 0))],
        out_shape=[jax.ShapeDtypeStruct((t, LANES), jnp.int32), jax.ShapeDtypeStruct((t, LANES), F32),
                   jax.ShapeDtypeStruct((8, LANES), jnp.int32)],
        scratch_shapes=[pltpu.VMEM((1, LANES), F32), pltpu.VMEM((1, LANES), F32), pltpu.VMEM((1, LANES), F32)],
        compiler_params=_cparams(("arbitrary", "arbitrary")),
        name="moe_route",
    )(logits)


def _row_copy(src, si, dst, di, sem):
    return pltpu.make_async_copy(src.at[pl.ds(si, 1), :], dst.at[pl.ds(di, 1), :], sem)


def _dispatch_kernel(cnt_ref, pst_ref, nu_ref, pos_ref, x_ref, xs_ref, zero_scr, sem, *, n_experts, tm, tt):
    i = pl.program_id(0)
    p_rows = xs_ref.shape[0]

    @pl.when(i == 0)
    def _():
        zero_scr[...] = jnp.zeros_like(zero_scr)
        zrows = zero_scr.shape[0]

        def pad_copy(e):
            start = jnp.minimum((pst_ref[e] + cnt_ref[e]) // SUBLANES * SUBLANES, p_rows - zrows)
            return pltpu.make_async_copy(zero_scr, xs_ref.at[pl.ds(pl.multiple_of(start, SUBLANES), zrows), :], sem)

        def start_pad(e, c):
            pad_copy(e).start()
            return c

        def wait_pad(e, c):
            pad_copy(e).wait()
            return c

        lax.fori_loop(0, n_experts, start_pad, 0)
        lax.fori_loop(0, n_experts, wait_pad, 0)

        def zero_tail(blk, c):
            cp = pltpu.make_async_copy(zero_scr.at[pl.ds(0, tm), :],
                                       xs_ref.at[pl.ds(pl.multiple_of(blk * tm, tm), tm), :], sem)
            cp.start()
            cp.wait()
            return c

        lax.fori_loop(nu_ref[0], p_rows // tm, zero_tail, 0)

    def start_rows(r8, c):
        for rr in range(SUBLANES):
            for k in range(TOP_K):
                p = pos_ref[(r8 * SUBLANES + rr) * TOP_K + k]
                _row_copy(x_ref.at[r8], rr, xs_ref, p, sem).start()
        return c

    def wait_rows(r, c):
        for k in range(TOP_K):
            _row_copy(x_ref.at[0], 0, xs_ref, 0, sem).wait()
        return c

    lax.fori_loop(0, tt // SUBLANES, start_rows, 0)
    lax.fori_loop(0, tt, wait_rows, 0, unroll=2)


def moe_dispatch(xpk, pos_flat, cnt, pst, n_used, p_rows, *, n_experts, tm, tt):
    t, dh = xpk.shape
    kern = functools.partial(_dispatch_kernel, n_experts=n_experts, tm=tm, tt=tt)
    return pl.pallas_call(
        kern,
        grid_spec=pltpu.PrefetchScalarGridSpec(
            num_scalar_prefetch=3,
            grid=(t // tt,),
            in_specs=[pl.BlockSpec((tt * TOP_K,), lambda i, *_: (i,), memory_space=pltpu.SMEM),
                      pl.BlockSpec((tt // SUBLANES, SUBLANES, dh), lambda i, *_: (i, 0, 0))],
            out_specs=pl.BlockSpec(memory_space=pl.ANY),
            scratch_shapes=[pltpu.VMEM((tm + SUBLANES, dh), U32), pltpu.SemaphoreType.DMA(())],
        ),
        out_shape=jax.ShapeDtypeStruct((p_rows, dh), U32),
        compiler_params=_cparams(("arbitrary",)),
        name="moe_dispatch",
    )(cnt, pst, n_used, pos_flat, xpk.reshape(t // SUBLANES, SUBLANES, dh))


def _new_expert(be_ref, i):
    prev = be_ref[jnp.maximum(i - 1, 0)]
    return (i == 0) | (be_ref[i] != prev)


def _by_block_rows(rows, tm, compute, out_ref):
    half = tm // 2

    def part():
        compute(half)
        out_ref[half:, :] = jnp.zeros((tm - half, out_ref.shape[1]), out_ref.dtype)

    pl.when(rows > half)(functools.partial(compute, tm))
    pl.when((rows > 0) & (rows <= half))(part)

    @pl.when(rows == 0)
    def _():
        out_ref[...] = jnp.zeros_like(out_ref)


def _moe_gu_kernel(be_ref, br_ref, x_ref, wg_ref, wu_ref, bg_ref, bu_ref, a_ref, wgb, wub):
    i = pl.program_id(1)
    tm, hw = x_ref.shape

    @pl.when(_new_expert(be_ref, i))
    def _():
        wgb[...] = wg_ref[...].astype(BF16)
        wub[...] = wu_ref[...].astype(BF16)

    def compute(n):
        lo, hi = _unpack_halves(x_ref[:n, :])
        gate = _dot(lo, wgb[:hw, :]) + _dot(hi, wgb[hw:, :]) + bg_ref[...]
        up = _dot(lo, wub[:hw, :]) + _dot(hi, wub[hw:, :]) + bu_ref[...]
        gate = jnp.minimum(gate, SWIGLU_LIMIT)
        up = jnp.clip(up, -SWIGLU_LIMIT, SWIGLU_LIMIT)
        sig = 1.0 / (1.0 + jnp.exp(-SWIGLU_ALPHA * gate))
        a_ref[:n, :] = (gate * sig * (up + 1.0)).astype(a_ref.dtype)

    _by_block_rows(br_ref[i], tm, compute, a_ref)


def _used_block(i, br):
    return jnp.where(br[i] > 0, i, 0)


def moe_gate_up(xs, w_gu, b_gu, layer, blk_e, blk_rows, *, tm, tn):
    p, hw = xs.shape
    d = 2 * hw
    f = w_gu.shape[3] // 2
    nj = f // tn
    wspec = lambda off: pl.BlockSpec((None, None, d, tn), lambda j, i, be, nu: (layer, be[i], 0, off + j))
    bspec = lambda off: pl.BlockSpec((None, None, 1, tn), lambda j, i, be, nu: (layer, be[i], 0, off + j))
    return pl.pallas_call(
        _moe_gu_kernel,
        grid_spec=pltpu.PrefetchScalarGridSpec(
            num_scalar_prefetch=2,
            grid=(nj, p // tm),
            in_specs=[pl.BlockSpec((tm, hw), lambda j, i, be, nu: (_used_block(i, nu), 0)),
                      wspec(0), wspec(nj), bspec(0), bspec(nj)],
            out_specs=pl.BlockSpec((tm, tn), lambda j, i, be, nu: (i, j)),
            scratch_shapes=[pltpu.VMEM((d, tn), BF16), pltpu.VMEM((d, tn), BF16)],
        ),
        out_shape=jax.ShapeDtypeStruct((p, f), BF16),
        compiler_params=_cparams(("arbitrary", "arbitrary")),
        name="moe_gate_up",
    )(blk_e, blk_rows, xs, w_gu, w_gu, b_gu, b_gu)


def _moe_down_kernel(be_ref, br_ref, a_ref, w_ref, b_ref, y_ref, wb):
    i = pl.program_id(1)

    @pl.when(_new_expert(be_ref, i))
    def _():
        wb[...] = w_ref[...].astype(BF16)

    def compute(n):
        y_ref[:n, :] = _dot(a_ref[:n, :], wb[...]) + b_ref[...]

    _by_block_rows(br_ref[i], a_ref.shape[0], compute, y_ref)


def moe_down(a, w_down, b_down, layer, blk_e, blk_rows, *, tm, tn):
    p, f = a.shape
    d = w_down.shape[3]
    return pl.pallas_call(
        _moe_down_kernel,
        grid_spec=pltpu.PrefetchScalarGridSpec(
            num_scalar_prefetch=2,
            grid=(d // tn, p // tm),
            in_specs=[pl.BlockSpec((tm, f), lambda j, i, be, nu: (i, 0)),
                      pl.BlockSpec((None, None, f, tn), lambda j, i, be, nu: (layer, be[i], 0, j)),
                      pl.BlockSpec((None, None, 1, tn), lambda j, i, be, nu: (layer, be[i], 0, j))],
            out_specs=pl.BlockSpec((tm, tn), lambda j, i, be, nu: (i, j)),
            scratch_shapes=[pltpu.VMEM((f, tn), BF16)],
        ),
        out_shape=jax.ShapeDtypeStruct((p, d), F32),
        compiler_params=_cparams(("arbitrary", "arbitrary")),
        name="moe_down",
    )(blk_e, blk_rows, a, w_down, b_down)


def _combine_kernel(pos_ref, posn_ref, gate_ref, x_ref, g_ref, b_ref, ys_ref, o_ref, ob_ref, buf, sem,
                    *, alpha, tt):
    i = pl.program_id(0)
    n = pl.num_programs(0)
    slot = i % 2

    def start_tile(p_ref, s):
        def body(r8, c):
            for rr in range(SUBLANES):
                for k in range(TOP_K):
                    p = p_ref[(r8 * SUBLANES + rr) * TOP_K + k]
                    _row_copy(ys_ref, p, buf.at[s, k, r8], rr, sem.at[s]).start()
            return c
        lax.fori_loop(0, tt // SUBLANES, body, 0)

    @pl.when(i == 0)
    def _():
        start_tile(pos_ref, 0)

    @pl.when(i + 1 < n)
    def _():
        start_tile(posn_ref, 1 - slot)

    def wait_rows(r, c):
        for k in range(TOP_K):
            _row_copy(ys_ref, 0, buf.at[slot, k, 0], 0, sem.at[slot]).wait()
        return c

    lax.fori_loop(0, tt, wait_rows, 0, unroll=2)
    gates = gate_ref[...]
    d = x_ref.shape[1]
    f = gates[:, 0:1] * buf[slot, 0].reshape(tt, d)
    for k in range(1, TOP_K):
        f = f + gates[:, k:k + 1] * buf[slot, k].reshape(tt, d)
    y = _layer_norm(alpha * x_ref[...] + f, g_ref[...], b_ref[...])
    o_ref[...] = y
    ob_ref[...] = y.astype(BF16)


def moe_combine_ln(ys, pos_flat, gates, x, g, b, *, alpha, tt):
    t, d = x.shape
    nt = t // tt
    tile = pl.BlockSpec((tt, d), lambda i: (i, 0))
    row = pl.BlockSpec((1, d), lambda i: (0, 0))
    kern = functools.partial(_combine_kernel, alpha=alpha, tt=tt)
    return pl.pallas_call(
        kern,
        grid=(nt,),
        in_specs=[pl.BlockSpec((tt * TOP_K,), lambda i: (i,), memory_space=pltpu.SMEM),
                  pl.BlockSpec((tt * TOP_K,), lambda i: (jnp.minimum(i + 1, nt - 1),), memory_space=pltpu.SMEM),
                  pl.BlockSpec((tt, LANES), lambda i: (i, 0)),
                  tile, row, row,
                  pl.BlockSpec(memory_space=pl.ANY)],
        out_specs=[tile, tile],
        out_shape=[jax.ShapeDtypeStruct((t, d), F32), jax.ShapeDtypeStruct((t, d), BF16)],
        scratch_shapes=[pltpu.VMEM((2, TOP_K, tt // SUBLANES, SUBLANES, d), F32),
                        pltpu.SemaphoreType.DMA((2,))],
        compiler_params=_cparams(("arbitrary",)),
        name="moe_combine_ln",
    )(pos_flat, pos_flat, gates, x, g, b, ys)


def kernel(x_prompt, x_sample, mem_prompt, mem_sample, ev_w_in, ev_w_out, ev_sink, ev_lam_q1, ev_lam_k1, ev_lam_q2, ev_lam_k2, ev_dnorm_w, od_w_in, od_w_out, od_conv_w, mem_wq, mem_wkv, mem_wo, ln_g, ln_b, moe_router_w, moe_router_b, moe_w_gu, moe_b_gu, moe_w_down, moe_b_down):
    b1, s1, d = x_prompt.shape
    b2, s2, _ = x_sample.shape
    segs = Segs(b1, s1, b2, s2)
    t = segs.t
    depth = ln_g.shape[0]
    n_mem = mem_prompt.shape[1]
    n_experts = moe_router_w.shape[2]
    alpha = (2 * depth) ** 0.25
    n_heads = d // (2 * HEAD_DIM)
    n_kv = n_heads // 4
    c_ch = d // 2
    dils = tuple(dil for _, dil in D_PATTERNS)

    x = jnp.concatenate([x_prompt.reshape(-1, d), x_sample.reshape(-1, d)], axis=0)
    xb = x.astype(BF16)
    memb = jnp.concatenate([mem_prompt.reshape(-1, d), mem_sample.reshape(-1, d)], axis=0).astype(BF16)
    tables = rope_tables(max(s1, s2))
    ev_modes = jnp.array([MODE_ROPE128_Q] * n_heads + [MODE_ROPE128] * n_kv + [MODE_NONE] * n_kv
                         + [MODE_ROPE64_Q] * n_heads + [MODE_ROPE64] * n_heads
                         + [MODE_NONE] * n_heads, jnp.int32)
    od_modes = jnp.zeros((3 * c_ch // LANES,), jnp.int32)
    mem_modes = jnp.zeros((mem_wkv.shape[2] // LANES,), jnp.int32)
    mem_segs = Segs(b1, n_mem, b2, n_mem)
    rw_pad = jnp.pad(moe_router_w, ((0, 0), (0, 0), (0, LANES - n_experts)))
    rb_pad = jnp.pad(moe_router_b, ((0, 0), (0, LANES - n_experts)))[:, None, :]
    b_gu4 = moe_b_gu[:, :, None, :]
    b_down4 = moe_b_down[:, :, None, :]
    moe_tm = 512
    n_blocks = -(-(t * TOP_K + n_experts * (moe_tm - 1)) // moe_tm)
    p_rows = n_blocks * moe_tm

    for l in range(depth):
        j = l // 2
        if l % 2 == 0:
            lambda_init = 0.8 - 0.6 * math.exp(-0.3 * l)
            w_in = ev_w_in[j].astype(BF16)
            n_in = w_in.shape[1]
            tn = n_in // 6 if (n_in // 6) % LANES == 0 else n_in // 18
            h = proj(xb, w_in, ev_modes, tables, segs, col0=0, ncols=n_in, tm=512, tn=tn,
                     out_dtype=BF16)
            lamv = jnp.stack([ev_lam_q1[j], ev_lam_k1[j], ev_lam_q2[j], ev_lam_k2[j]])
            oa, ob = [], []
            for row0, bsz, s in segs.groups():
                oa.append(window_attention(h, ev_sink[j], row0, bsz, s, d_model=d))
                ob.append(diff_attention(h, lamv, ev_dnorm_w[j][None, :], row0, bsz, s, lambda_init,
                                         d_model=d, tq=1024, tk=2048, n_streams=4))
            fs = [jnp.concatenate(oa, axis=0), jnp.concatenate(ob, axis=0)]
            w_out = ev_w_out[j].astype(BF16)
        else:
            w_in = od_w_in[j].astype(BF16)
            hc = proj(xb, w_in, od_modes, tables, segs, col0=0, ncols=3 * c_ch, tm=512,
                      tn=c_ch, out_dtype=F32)
            hqs = [proj_dilated(xb, w_in, tables, segs, g, dil, col0=3 * c_ch, tm=512)
                   for g, dil in enumerate(dils)]
            oc, od = [], []
            for row0, bsz, s in segs.groups():
                oc.append(gated_conv(hc, od_conv_w[j], row0, bsz, s, ts=512, tc=c_ch // 2))
                outs = [dilated_group(hq, dil, row0, bsz, s) for hq, dil in zip(hqs, dils)]
                od.append(dilated_combine([o for o, _ in outs], [ls for _, ls in outs], dils, tm=512))
            fs = [jnp.concatenate(oc, axis=0), jnp.concatenate(od, axis=0)]
            w_out = od_w_out[j].astype(BF16)
        x, xb = outproj_ln(fs, w_out, x, ln_g[l, 0][None, :], ln_b[l, 0][None, :], alpha=alpha, tm=256)

        kv = proj(memb, mem_wkv[l].astype(BF16), mem_modes, tables, mem_segs, col0=0,
                  ncols=mem_wkv.shape[2], tm=n_mem, tn=mem_wkv.shape[2] // 2, out_dtype=BF16)
        x, xpk, logits = mem_attention_ln(xb, x, kv, mem_wq[l].astype(BF16), mem_wo[l].astype(BF16),
                                          ln_g[l, 1][None, :], ln_b[l, 1][None, :], rw_pad[l], rb_pad[l],
                                          segs, alpha=alpha, tm=512, n_mem=n_mem)

        pos, gates, meta = route(logits, n_experts, tm=moe_tm, tt=512)
        pos_flat = pos[:, :TOP_K].reshape(-1)
        cnt, pst, end_blk = meta[0, :n_experts], meta[1, :n_experts], meta[2, :n_experts]
        blk_e = jnp.minimum(jnp.sum(end_blk[None, :] <= jnp.arange(n_blocks, dtype=jnp.int32)[:, None], axis=1),
                            n_experts - 1).astype(jnp.int32)
        n_used = end_blk[n_experts - 1:]
        blk_lo = (jnp.arange(n_blocks, dtype=jnp.int32) * moe_tm)[:, None]
        blk_rows = jnp.sum(jnp.clip(jnp.minimum((pst + cnt)[None, :], blk_lo + moe_tm)
                                    - jnp.maximum(pst[None, :], blk_lo), 0, moe_tm), axis=1).astype(jnp.int32)
        xs = moe_dispatch(xpk, pos_flat, cnt, pst, n_used, p_rows, n_experts=n_experts, tm=moe_tm, tt=256)
        a = moe_gate_up(xs, moe_w_gu, b_gu4, l, blk_e, blk_rows, tm=moe_tm, tn=1024)
        ys = moe_down(a, moe_w_down, b_down4, l, blk_e, blk_rows, tm=moe_tm, tn=1024)
        x, xb = moe_combine_ln(ys, pos_flat, gates, x, ln_g[l, 2][None, :], ln_b[l, 2][None, :],
                               alpha=alpha, tt=128)

    y1 = x[:segs.t1].reshape(b1, s1, d)
    y2 = x[segs.t1:].reshape(b2, s2, d)
    return (y1, y2)
```

```python
import functools
import math

import jax
import jax.numpy as jnp
from jax import lax
from jax.experimental import pallas as pl
from jax.experimental.pallas import tpu as pltpu

F32 = jnp.float32
BF16 = jnp.bfloat16
U32 = jnp.uint32
LANES = 128
SUBLANES = 8
VMEM_LIMIT = 56 * 1024 * 1024

HEAD_DIM = 128
ROPE_THETA = 10000.0
A_WINDOW = 128
B_DIM = 64
C_WIDTH = 3
D_PATTERNS = ((128, 1), (512, 4), (2048, 16))
D_HEADS_PER_GROUP = 4
D_NK = 64
MEM_HEADS = 4
TOP_K = 4
SWIGLU_LIMIT = 7.0
SWIGLU_ALPHA = 1.702
LN_EPS = 1e-5
NEG_INF = -1e30

MODE_NONE, MODE_ROPE128, MODE_ROPE128_Q, MODE_ROPE64, MODE_ROPE64_Q = 0, 1, 2, 3, 4


def _cparams(sem):
    return pltpu.CompilerParams(dimension_semantics=sem, vmem_limit_bytes=VMEM_LIMIT)


def _dot(a, b):
    return jnp.dot(a, b, preferred_element_type=F32)


def _dot_t(a, b):
    return lax.dot_general(a, b, (((1,), (1,)), ((), ())), preferred_element_type=F32)


class Segs:
    def __init__(self, b1, s1, b2, s2):
        self.b1, self.s1, self.b2, self.s2 = b1, s1, b2, s2
        self.t1 = b1 * s1
        self.t = b1 * s1 + b2 * s2

    def groups(self):
        return ((0, self.b1, self.s1), (self.t1, self.b2, self.s2))

    def pos_tile(self, i, tm):
        n1 = self.t1 // tm
        return jnp.where(i < n1, i % (self.s1 // tm), (i - n1) % (self.s2 // tm))

    def batch_of_tile(self, i, tm):
        n1 = self.t1 // tm
        return jnp.where(i < n1, (i * tm) // self.s1, self.b1 + ((i - n1) * tm) // self.s2)


def _layer_norm(z, g, b):
    mu = jnp.mean(z, axis=-1, keepdims=True)
    zc = z - mu
    var = jnp.mean(zc * zc, axis=-1, keepdims=True)
    return zc * lax.rsqrt(var + LN_EPS) * g + b


def _rope128(a, c, s):
    return a * c + pltpu.roll(a, 64, 1) * s


def _proj_kernel(mode_ref, x_ref, w_ref, c128_ref, s128_ref, c64_ref, s64_ref, o_ref,
                 *, n_slabs, slab0, q128_scale, q64_scale):
    j = pl.program_id(0)
    acc = _dot(x_ref[...], w_ref[...])
    for s in range(n_slabs):
        a = acc[:, s * LANES:(s + 1) * LANES]
        mode = mode_ref[slab0 + j * n_slabs + s]
        w0 = jnp.where(mode == MODE_NONE, 1.0, 0.0)
        w128 = jnp.where(mode == MODE_ROPE128, 1.0, jnp.where(mode == MODE_ROPE128_Q, q128_scale, 0.0))
        w64 = jnp.where(mode == MODE_ROPE64, 1.0, jnp.where(mode == MODE_ROPE64_Q, q64_scale, 0.0))
        c = w0 + w128 * c128_ref[...] + w64 * c64_ref[...]
        sn = w128 * s128_ref[...] + w64 * s64_ref[...]
        o_ref[:, s * LANES:(s + 1) * LANES] = _rope128(a, c, sn).astype(o_ref.dtype)


def proj(xb, w, modes, tables, segs, *, col0, ncols, tm, tn, out_dtype):
    m, k = xb.shape
    assert ncols % tn == 0 and col0 % tn == 0 and m % tm == 0 and tn % LANES == 0
    n_slabs = tn // LANES
    jb = col0 // tn
    tab_spec = pl.BlockSpec((tm, LANES), lambda j, i, *_: (segs.pos_tile(i, tm), 0))
    kern = functools.partial(_proj_kernel, n_slabs=n_slabs, slab0=col0 // LANES,
                             q128_scale=HEAD_DIM ** -0.5, q64_scale=B_DIM ** -0.5)
    return pl.pallas_call(
        kern,
        grid_spec=pltpu.PrefetchScalarGridSpec(
            num_scalar_prefetch=1,
            grid=(ncols // tn, m // tm),
            in_specs=[pl.BlockSpec((tm, k), lambda j, i, *_: (i, 0)),
                      pl.BlockSpec((k, tn), lambda j, i, *_: (0, jb + j)),
                      tab_spec, tab_spec, tab_spec, tab_spec],
            out_specs=pl.BlockSpec((tm, tn), lambda j, i, *_: (i, j)),
        ),
        out_shape=jax.ShapeDtypeStruct((m, ncols), out_dtype),
        compiler_params=_cparams(("arbitrary", "arbitrary")),
        name="proj",
    )(modes, xb, w, *tables)


def rope_tables(s):
    inv = ROPE_THETA ** (-jnp.arange(0, HEAD_DIM, 2, dtype=F32) / HEAD_DIM)
    ang = jnp.arange(s, dtype=F32)[:, None] * inv[None, :]
    c, sn = jnp.cos(ang), jnp.sin(ang)
    c128 = jnp.concatenate([c, c], axis=1)
    s128 = jnp.concatenate([-sn, sn], axis=1)
    invb = ROPE_THETA ** (-jnp.arange(0, B_DIM, 2, dtype=F32) / B_DIM)
    angb = jnp.arange(s, dtype=F32)[:, None] * invb[None, :]
    cb, sb = jnp.cos(angb), jnp.sin(angb)
    c64 = jnp.concatenate([cb, cb, cb, cb], axis=1)
    s64 = jnp.concatenate([-sb, -sb, sb, sb], axis=1)
    return (c128, s128, c64, s64)


B_SLAB_PERM = tuple(list(range(0, 32)) + list(range(64, 96)) + list(range(32, 64)) + list(range(96, 128)))


def _is_comp0(lane):
    return (lane // (B_DIM // 2)) % 2 == 0


def _window_kernel(sink_ref, q_ref, kp_ref, kc_ref, kn_ref, vp_ref, vc_ref, vn_ref, o_ref,
                   *, n_heads, n_kv, blk):
    n = pl.program_id(1)
    nb = pl.num_programs(1)
    r = lax.broadcasted_iota(jnp.int32, (blk, 3 * blk), 0)
    c = lax.broadcasted_iota(jnp.int32, (blk, 3 * blk), 1)
    valid = jnp.abs(r + blk - c) <= A_WINDOW
    valid &= (c >= blk) | (n > 0)
    valid &= (c < 2 * blk) | (n < nb - 1)
    g = n_heads // n_kv
    for kv in range(n_kv):
        cs = slice(kv * HEAD_DIM, (kv + 1) * HEAD_DIM)
        k3 = jnp.concatenate([kp_ref[:, cs], kc_ref[:, cs], kn_ref[:, cs]], axis=0)
        v3 = jnp.concatenate([vp_ref[:, cs], vc_ref[:, cs], vn_ref[:, cs]], axis=0)
        for gi in range(g):
            h = kv * g + gi
            hs = slice(h * HEAD_DIM, (h + 1) * HEAD_DIM)
            sc = jnp.where(valid, _dot_t(q_ref[:, hs], k3), NEG_INF)
            sk = sink_ref[h]
            m = jnp.maximum(jnp.max(sc, axis=-1, keepdims=True), sk)
            p = jnp.exp(sc - m)
            denom = jnp.sum(p, axis=-1, keepdims=True) + jnp.exp(sk - m)
            o = _dot(p.astype(BF16), v3) / denom
            o_ref[:, hs] = o.astype(o_ref.dtype)


def window_attention(h, sink, row0, bsz, s, *, d_model):
    blk = A_WINDOW
    n_heads = d_model // (2 * HEAD_DIM)
    n_kv = n_heads // 4
    qw, kw = n_heads * HEAD_DIM, n_kv * HEAD_DIM
    nb = s // blk
    rb0 = row0 // blk
    kcol, vcol = qw // kw, qw // kw + 1

    def rows(b, n):
        return rb0 + b * nb + n

    def spec(col, shift):
        def imap(b, n):
            nn = jnp.clip(n + shift, 0, nb - 1)
            return (rows(b, nn), col)
        return pl.BlockSpec((blk, kw), imap)

    kern = functools.partial(_window_kernel, n_heads=n_heads, n_kv=n_kv, blk=blk)
    return pl.pallas_call(
        kern,
        grid=(bsz, nb),
        in_specs=[pl.BlockSpec(memory_space=pltpu.SMEM),
                  pl.BlockSpec((blk, qw), lambda b, n: (rows(b, n), 0)),
                  spec(kcol, -1), spec(kcol, 0), spec(kcol, 1),
                  spec(vcol, -1), spec(vcol, 0), spec(vcol, 1)],
        out_specs=pl.BlockSpec((blk, qw), lambda b, n: (b * nb + n, 0)),
        out_shape=jax.ShapeDtypeStruct((bsz * s, qw), BF16),
        compiler_params=_cparams(("arbitrary", "arbitrary")),
        name="window_attn",
    )(sink, h, h, h, h, h, h, h)


def _diff_kernel(lamv_ref, nw_ref, q_ref, k_ref, v_ref, o_ref, *, tq, tk, n_streams, lambda_init):
    s = k_ref.shape[0]
    ts = tq // n_streams
    lane = lax.broadcasted_iota(jnp.int32, (ts, LANES), 1)
    q2s = []
    for st in range(n_streams):
        q = q_ref[st * ts:(st + 1) * ts, :]
        zero = jnp.zeros_like(q)
        q2s.append(jnp.concatenate([jnp.where(_is_comp0(lane), q, zero),
                                    jnp.where(_is_comp0(lane), zero, q)], axis=0))

    def body(i, carry):
        ks = pl.multiple_of(i * tk, tk)
        kb = k_ref[pl.ds(ks, tk), :]
        vb = v_ref[pl.ds(ks, tk), :]
        out = []
        for st in range(n_streams):
            m, l, acc = carry[st]
            sc = _dot_t(q2s[st], kb)
            m_new = jnp.maximum(m, jnp.max(sc, axis=-1, keepdims=True))
            a = jnp.exp(m - m_new)
            p = jnp.exp(sc - m_new)
            l = a * l + jnp.sum(p, axis=-1, keepdims=True)
            acc = a * acc + _dot(p.astype(BF16), vb)
            out.append((m_new, l, acc))
        return tuple(out)

    init = tuple((jnp.full((2 * ts, 1), NEG_INF, F32), jnp.zeros((2 * ts, 1), F32),
                  jnp.zeros((2 * ts, LANES), F32)) for _ in range(n_streams))
    res = lax.fori_loop(0, s // tk, body, init)
    lv = lamv_ref[...]
    lam = (jnp.exp(jnp.sum(lv[0:1] * lv[1:2], axis=-1, keepdims=True))
           - jnp.exp(jnp.sum(lv[2:3] * lv[3:4], axis=-1, keepdims=True)) + lambda_init)
    for st in range(n_streams):
        m, l, acc = res[st]
        on = acc / l
        of = on[:ts] - lam * on[ts:]
        of = of * lax.rsqrt(jnp.mean(of * of, axis=-1, keepdims=True) + LN_EPS) * nw_ref[...]
        o_ref[st * ts:(st + 1) * ts, :] = (of * (1.0 - lambda_init)).astype(o_ref.dtype)


def diff_attention(h, lamv, norm_w, row0, bsz, s, lambda_init, *, d_model, tq, tk, n_streams):
    n_heads = d_model // (2 * HEAD_DIM)
    a_cols = (n_heads + 2 * (n_heads // 4))
    qc, kc, vc = a_cols, a_cols + n_heads, a_cols + 2 * n_heads
    nq = s // tq
    assert row0 % s == 0
    sb0 = row0 // s
    qb0 = row0 // tq
    kern = functools.partial(_diff_kernel, tq=tq, tk=tk, n_streams=n_streams, lambda_init=lambda_init)
    return pl.pallas_call(
        kern,
        grid=(bsz, n_heads, nq),
        in_specs=[pl.BlockSpec((4, B_DIM), lambda b, hh, n: (0, 0)),
                  pl.BlockSpec((1, LANES), lambda b, hh, n: (0, 0)),
                  pl.BlockSpec((tq, LANES), lambda b, hh, n: (qb0 + b * nq + n, qc + hh)),
                  pl.BlockSpec((s, LANES), lambda b, hh, n: (sb0 + b, kc + hh)),
                  pl.BlockSpec((s, LANES), lambda b, hh, n: (sb0 + b, vc + hh))],
        out_specs=pl.BlockSpec((tq, LANES), lambda b, hh, n: (b * nq + n, hh)),
        out_shape=jax.ShapeDtypeStruct((bsz * s, n_heads * LANES), BF16),
        compiler_params=_cparams(("arbitrary", "arbitrary", "arbitrary")),
        name="diff_attn",
    )(lamv, norm_w, h, h, h)


def _conv_kernel(cw_ref, b_ref, c_ref, x_ref, cp_ref, xp_ref, cn_ref, xn_ref, o_ref, *, ts):
    n = pl.program_id(1)
    nt = pl.num_programs(1)
    u = c_ref[...] * x_ref[...]
    prev = jnp.where(n > 0, cp_ref[7:8, :] * xp_ref[7:8, :], 0.0)
    nxt = jnp.where(n < nt - 1, cn_ref[0:1, :] * xn_ref[0:1, :], 0.0)
    row = lax.broadcasted_iota(jnp.int32, u.shape, 0)
    um1 = jnp.where(row == 0, prev, pltpu.roll(u, 1, 0))
    up1 = jnp.where(row == ts - 1, nxt, pltpu.roll(u, ts - 1, 0))
    conv = um1 * cw_ref[0:1, :] + u * cw_ref[1:2, :] + up1 * cw_ref[2:3, :]
    o_ref[...] = (b_ref[...] * conv).astype(o_ref.dtype)


def gated_conv(hc, conv_w, row0, bsz, s, *, ts, tc):
    c = hc.shape[1] // 3
    nt = s // ts
    nc = c // tc
    rb0 = row0 // ts
    r8 = ts // 8

    def main(col):
        return pl.BlockSpec((ts, tc), lambda b, n, j: (rb0 + b * nt + n, col * nc + j))

    def halo(col, shift):
        def imap(b, n, j):
            blk8 = (rb0 + b * nt + n) * r8 + (-1 if shift < 0 else r8)
            return (jnp.clip(blk8, 0, hc.shape[0] // 8 - 1), col * nc + j)
        return pl.BlockSpec((8, tc), imap)

    kern = functools.partial(_conv_kernel, ts=ts)
    return pl.pallas_call(
        kern,
        grid=(bsz, nt, nc),
        in_specs=[pl.BlockSpec((C_WIDTH, tc), lambda b, n, j: (0, j)),
                  main(0), main(1), main(2), halo(1, -1), halo(2, -1), halo(1, 1), halo(2, 1)],
        out_specs=pl.BlockSpec((ts, tc), lambda b, n, j: (b * nt + n, j)),
        out_shape=jax.ShapeDtypeStruct((bsz * s, c), BF16),
        compiler_params=_cparams(("arbitrary", "arbitrary", "arbitrary")),
        name="gated_conv",
    )(conv_w, hc, hc, hc, hc, hc, hc, hc)


def _proj_dil_kernel(x_ref, w_ref, c_ref, s_ref, o_ref, *scrs, dil, tm):
    rows = tm // dil
    acc = _dot(x_ref[...], w_ref[...])
    for sl, scr in enumerate(scrs):
        cols = slice(sl * LANES, (sl + 1) * LANES)
        a = _rope128(acc[:, cols], c_ref[...], s_ref[...])
        if dil == 1:
            o_ref[0, :, cols] = a.astype(o_ref.dtype)
        else:
            scr[...] = a
    if dil > 1:
        for r in range(dil):
            for sl, scr in enumerate(scrs):
                o_ref[r, :, sl * LANES:(sl + 1) * LANES] = scr[pl.ds(r, rows, stride=dil), :].astype(o_ref.dtype)


def dilated_tables(tables):
    c128, s128 = tables[0], tables[1]
    q_scale = HEAD_DIM ** -0.5
    return (jnp.stack([c128 * q_scale, c128, jnp.ones_like(c128)]),
            jnp.stack([s128 * q_scale, s128, jnp.zeros_like(s128)]))


def proj_dilated(xb, w, dtabs, segs, g, dil, *, col0, tm):
    m, k = xb.shape
    gw = D_HEADS_PER_GROUP * HEAD_DIM
    n_groups = len(D_PATTERNS)
    cb0 = col0 // gw
    tab = pl.BlockSpec((None, tm, LANES), lambda j, i: (j, segs.pos_tile(i, tm), 0))
    kern = functools.partial(_proj_dil_kernel, dil=dil, tm=tm)
    return pl.pallas_call(
        kern,
        grid=(3, m // tm),
        in_specs=[pl.BlockSpec((tm, k), lambda j, i: (i, 0)),
                  pl.BlockSpec((k, gw), lambda j, i: (0, cb0 + j * n_groups + g)),
                  tab, tab],
        out_specs=pl.BlockSpec((dil, tm // dil, gw), lambda j, i: (0, i, j)),
        out_shape=jax.ShapeDtypeStruct((dil, m // dil, 3 * gw), BF16),
        scratch_shapes=[pltpu.VMEM((tm, LANES), F32) for _ in range(gw // LANES)],
        compiler_params=_cparams(("arbitrary", "arbitrary")),
        name="proj_dilated",
    )(xb, w, dtabs[0], dtabs[1])


def _dilated_kernel(q_ref, kp_ref, kc_ref, kn_ref, vp_ref, vc_ref, vn_ref, o_ref, lse_ref,
                    *, blk, half):
    n = pl.program_id(2)
    nt = pl.num_programs(2)
    wk = blk + 2 * half
    r = lax.broadcasted_iota(jnp.int32, (blk, wk), 0)
    c = lax.broadcasted_iota(jnp.int32, (blk, wk), 1)
    valid = (c >= r) & (c <= r + 2 * half)
    valid &= (c >= half) | (n > 0)
    valid &= (c < half + blk) | (n < nt - 1)
    lane = lax.broadcasted_iota(jnp.int32, (blk, LANES), 1)
    lse_all = jnp.zeros((blk, LANES), F32)
    per = LANES // D_HEADS_PER_GROUP
    for hh in range(D_HEADS_PER_GROUP):
        hs = slice(hh * HEAD_DIM, (hh + 1) * HEAD_DIM)
        kk = jnp.concatenate([kp_ref[:, hs], kc_ref[:, hs], kn_ref[:, hs]], axis=0)
        vv = jnp.concatenate([vp_ref[:, hs], vc_ref[:, hs], vn_ref[:, hs]], axis=0)
        sc = jnp.where(valid, _dot_t(q_ref[:, hs], kk), NEG_INF)
        m = jnp.max(sc, axis=-1, keepdims=True)
        p = jnp.exp(sc - m)
        l = jnp.sum(p, axis=-1, keepdims=True)
        o_ref[:, hs] = _dot(p.astype(BF16), vv) / l
        lse_all = jnp.where(lane // per == hh, m + jnp.log(l), lse_all)
    lse_ref[...] = lse_all


def dilated_group(hq, dil, row0, bsz, s):
    blk, half = 2 * D_NK, D_NK
    gw = D_HEADS_PER_GROUP * HEAD_DIM
    sd = s // dil
    nt = sd // blk
    assert (row0 // dil) % blk == 0 and sd % blk == 0
    rb0 = row0 // dil // blk
    hb = blk // half

    def cur(sec):
        return pl.BlockSpec((None, blk, gw), lambda b, r, n: (r, rb0 + b * nt + n, sec))

    def halo(sec, shift):
        def imap(b, r, n):
            nn = jnp.clip(n * hb + (-1 if shift < 0 else hb), 0, nt * hb - 1)
            return (r, (rb0 + b * nt) * hb + nn, sec)
        return pl.BlockSpec((None, half, gw), imap)

    kern = functools.partial(_dilated_kernel, blk=blk, half=half)
    return pl.pallas_call(
        kern,
        grid=(bsz, dil, nt),
        in_specs=[cur(0), halo(1, -1), cur(1), halo(1, 1), halo(2, -1), cur(2), halo(2, 1)],
        out_specs=[pl.BlockSpec((None, blk, gw), lambda b, r, n: (r, b * nt + n, 0)),
                   pl.BlockSpec((None, blk, LANES), lambda b, r, n: (r, b * nt + n, 0))],
        out_shape=[jax.ShapeDtypeStruct((dil, bsz * sd, gw), F32),
                   jax.ShapeDtypeStruct((dil, bsz * sd, LANES), F32)],
        compiler_params=_cparams(("arbitrary", "arbitrary", "arbitrary")),
        name="dilated_attn",
    )(hq, hq, hq, hq, hq, hq, hq)


def _dil_combine_kernel(*refs, dils, tm):
    ng = len(dils)
    nh = D_HEADS_PER_GROUP
    o_refs, l_refs = refs[:ng], refs[ng:2 * ng]
    out_ref = refs[2 * ng]
    scr = refs[2 * ng + 1:]
    o_scr = [scr[g * (nh + 1):g * (nh + 1) + nh] for g in range(ng)]
    l_scr = [scr[g * (nh + 1) + nh] for g in range(ng)]
    for o_ref, l_ref, osc, lsc, dil in zip(o_refs, l_refs, o_scr, l_scr, dils):
        for r in range(dil):
            rs = pl.ds(r, tm // dil, stride=dil) if dil > 1 else pl.ds(0, tm)
            for hh in range(nh):
                osc[hh][rs, :] = o_ref[r, :, hh * HEAD_DIM:(hh + 1) * HEAD_DIM]
            lsc[rs, :] = l_ref[r]
    per = LANES // nh
    for hh in range(nh):
        hs = slice(hh * HEAD_DIM, (hh + 1) * HEAD_DIM)
        ls = slice(hh * per, hh * per + 1)
        lses = [lsc[:, ls] for lsc in l_scr]
        m = functools.reduce(jnp.maximum, lses)
        ws = [jnp.exp(a - m) for a in lses]
        den = functools.reduce(lambda a, b: a + b, ws)
        o = functools.reduce(lambda a, b: a + b, [(w / den) * osc[hh][...] for w, osc in zip(ws, o_scr)])
        out_ref[:, hs] = o.astype(out_ref.dtype)


def dilated_combine(os_, lses, dils, *, tm):
    gw = os_[0].shape[2]
    m = os_[0].shape[0] * os_[0].shape[1]
    ospecs = [pl.BlockSpec((d, tm // d, gw), lambda i: (0, i, 0)) for d in dils]
    lspecs = [pl.BlockSpec((d, tm // d, LANES), lambda i: (0, i, 0)) for d in dils]
    kern = functools.partial(_dil_combine_kernel, dils=dils, tm=tm)
    return pl.pallas_call(
        kern,
        grid=(m // tm,),
        in_specs=ospecs + lspecs,
        out_specs=pl.BlockSpec((tm, gw), lambda i: (i, 0)),
        out_shape=jax.ShapeDtypeStruct((m, gw), BF16),
        scratch_shapes=[pltpu.VMEM((tm, LANES), F32) for _ in range(len(dils) * (D_HEADS_PER_GROUP + 1))],
        compiler_params=_cparams(("arbitrary",)),
        name="dilated_combine",
    )(*os_, *lses)


def _outproj_kernel(*refs, n_in, alpha):
    f_refs = refs[:n_in]
    w_refs = refs[n_in:2 * n_in]
    x_ref, g_ref, b_ref, o_ref, ob_ref = refs[2 * n_in:]
    acc = alpha * x_ref[...]
    for f_ref, w_ref in zip(f_refs, w_refs):
        acc = acc + _dot(f_ref[...], w_ref[...])
    y = _layer_norm(acc, g_ref[...], b_ref[...])
    o_ref[...] = y
    ob_ref[...] = y.astype(BF16)


def outproj_ln(fs, w, x, g, b, *, alpha, tm):
    m, d = x.shape
    n_in = len(fs)
    widths = [f.shape[1] for f in fs]
    in_specs = [pl.BlockSpec((tm, wd), lambda i: (i, 0)) for wd in widths]
    ws = []
    off = 0
    for wd in widths:
        assert off % wd == 0
        in_specs.append(pl.BlockSpec((wd, d), functools.partial(lambda i, o: (o, 0), o=off // wd)))
        ws.append(w)
        off += wd
    row = pl.BlockSpec((1, d), lambda i: (0, 0))
    in_specs += [pl.BlockSpec((tm, d), lambda i: (i, 0)), row, row]
    kern = functools.partial(_outproj_kernel, n_in=n_in, alpha=alpha)
    return pl.pallas_call(
        kern,
        grid=(m // tm,),
        in_specs=in_specs,
        out_specs=[pl.BlockSpec((tm, d), lambda i: (i, 0)), pl.BlockSpec((tm, d), lambda i: (i, 0))],
        out_shape=[jax.ShapeDtypeStruct((m, d), F32), jax.ShapeDtypeStruct((m, d), BF16)],
        compiler_params=_cparams(("arbitrary",)),
        name="outproj_ln",
    )(*fs, *ws, x, g, b)


def _pack_halves(y):
    bits = lax.bitcast_convert_type(y.astype(BF16).astype(F32), U32)
    hw = y.shape[1] // 2
    return (bits[:, hw:] & jnp.uint32(0xFFFF0000)) | (bits[:, :hw] >> 16)


def _unpack_halves(pk):
    lo = lax.bitcast_convert_type(pk << 16, F32).astype(BF16)
    hi = lax.bitcast_convert_type(pk & jnp.uint32(0xFFFF0000), F32).astype(BF16)
    return lo, hi


def _mem_kernel(xb_ref, x_ref, kv_ref, wq_ref, wo_ref, g_ref, b_ref, rw_ref, rb_ref,
                o_ref, pk_ref, lg_ref, *, alpha):
    inner = wq_ref.shape[1]
    hd = inner // MEM_HEADS
    q = (_dot(xb_ref[...], wq_ref[...]) * hd ** -0.5).astype(BF16)
    outs = []
    for hh in range(MEM_HEADS):
        hs = slice(hh * hd, (hh + 1) * hd)
        sc = _dot_t(q[:, hs], kv_ref[:, hs])
        m = jnp.max(sc, axis=-1, keepdims=True)
        p = jnp.exp(sc - m)
        l = jnp.sum(p, axis=-1, keepdims=True)
        outs.append(_dot(p.astype(BF16), kv_ref[:, inner + hh * hd:inner + (hh + 1) * hd]) / l)
    o = jnp.concatenate(outs, axis=1).astype(BF16)
    y = _layer_norm(alpha * x_ref[...] + _dot(o, wo_ref[...]), g_ref[...], b_ref[...])
    o_ref[...] = y
    pk_ref[...] = _pack_halves(y)
    yb = y.astype(BF16)
    yl = (y - yb.astype(F32)).astype(BF16)
    rw = rw_ref[...]
    rwh = rw.astype(BF16)
    rwl = (rw - rwh.astype(F32)).astype(BF16)
    lg_ref[...] = _dot(yb, rwh) + _dot(yl, rwh) + _dot(yb, rwl) + rb_ref[...]


def mem_attention_ln(xb, x, kv, wq, wo, g, b, rw, rb, segs, *, alpha, tm, n_mem):
    m, d = x.shape
    inner = wq.shape[1]
    full = lambda a: pl.BlockSpec(a.shape, lambda i: (0, 0))
    tile = pl.BlockSpec((tm, d), lambda i: (i, 0))
    kern = functools.partial(_mem_kernel, alpha=alpha)
    return pl.pallas_call(
        kern,
        grid=(m // tm,),
        in_specs=[tile, tile,
                  pl.BlockSpec((n_mem, 2 * inner), lambda i: (segs.batch_of_tile(i, tm), 0)),
                  full(wq), full(wo), full(g), full(b), full(rw), full(rb)],
        out_specs=[tile, pl.BlockSpec((tm, d // 2), lambda i: (i, 0)),
                   pl.BlockSpec((tm, LANES), lambda i: (i, 0))],
        out_shape=[jax.ShapeDtypeStruct((m, d), F32), jax.ShapeDtypeStruct((m, d // 2), U32),
                   jax.ShapeDtypeStruct((m, LANES), F32)],
        compiler_params=_cparams(("arbitrary",)),
        name="mem_attn_ln",
    )(xb, x, kv, wq, wo, g, b, rw, rb)


def _route_kernel(lg_ref, pos_ref, gate_ref, meta_ref, cnt_ref, run_ref, pst_ref,
                  *, n_experts, tm, tt):
    ph = pl.program_id(0)
    i = pl.program_id(1)
    lane = lax.broadcasted_iota(jnp.int32, (tt, LANES), 1)
    lanef = lane.astype(F32)
    work = jnp.where(lane < n_experts, lg_ref[...], -jnp.inf)
    vals, hots = [], []
    for _ in range(TOP_K):
        m = jnp.max(work, axis=-1, keepdims=True)
        idx = jnp.min(jnp.where(work == m, lanef, float(LANES)), axis=-1, keepdims=True)
        hot = lanef == idx
        vals.append(m)
        hots.append(hot)
        work = jnp.where(hot, -jnp.inf, work)
    sel = functools.reduce(jnp.logical_or, hots).astype(F32)
    colsum = jnp.sum(sel, axis=0, keepdims=True)

    @pl.when((ph == 0) & (i == 0))
    def _():
        cnt_ref[...] = jnp.zeros_like(cnt_ref)

    @pl.when(ph == 0)
    def _():
        cnt_ref[...] += colsum

    @pl.when((ph == 1) & (i == 0))
    def _():
        cnt = cnt_ref[...]
        nblk = jnp.floor((cnt + (tm - 1)) / tm)
        a = lax.broadcasted_iota(jnp.int32, (LANES, LANES), 0)
        b = lax.broadcasted_iota(jnp.int32, (LANES, LANES), 1)
        upper = (a < b).astype(BF16)
        first = _dot(jnp.broadcast_to(nblk, (8, LANES)).astype(BF16), upper)[0:1]
        pst_ref[...] = first * tm
        run_ref[...] = jnp.zeros_like(run_ref)
        row = lax.broadcasted_iota(jnp.int32, meta_ref.shape, 0)
        meta = jnp.where(row == 0, cnt, jnp.where(row == 1, first * tm, jnp.where(row == 2, first + nblk, 0.0)))
        meta_ref[...] = meta.astype(jnp.int32)

    @pl.when(ph == 1)
    def _():
        r = lax.broadcasted_iota(jnp.int32, (tt, tt), 0)
        c = lax.broadcasted_iota(jnp.int32, (tt, tt), 1)
        before = _dot((r > c).astype(BF16), sel.astype(BF16)) + run_ref[...]
        posfull = pst_ref[...] + before
        e = [jnp.exp(v - vals[0]) for v in vals]
        den = functools.reduce(lambda x, y: x + y, e)
        pos = jnp.zeros((tt, LANES), F32)
        gat = jnp.zeros((tt, LANES), F32)
        for k in range(TOP_K):
            pk = jnp.sum(jnp.where(hots[k], posfull, 0.0), axis=-1, keepdims=True)
            pos = jnp.where(lane == k, pk, pos)
            gat = jnp.where(lane == k, e[k] / den, gat)
        pos_ref[...] = pos.astype(jnp.int32)
        gate_ref[...] = gat
        run_ref[...] += colsum


def route(logits, n_experts, *, tm, tt):
    t = logits.shape[0]
    tile = lambda: pl.BlockSpec((tt, LANES), lambda p, i: (i * p, 0))
    kern = functools.partial(_route_kernel, n_experts=n_experts, tm=tm, tt=tt)
    return pl.pallas_call(
        kern,
        grid=(2, t // tt),
        in_specs=[pl.BlockSpec((tt, LANES), lambda p, i: (i, 0))],
        out_specs=[tile(), tile(), pl.BlockSpec((8, LANES), lambda p, i: (0, 0))],
        out_shape=[jax.ShapeDtypeStruct((t, LANES), jnp.int32), jax.ShapeDtypeStruct((t, LANES), F32),
                   jax.ShapeDtypeStruct((8, LANES), jnp.int32)],
        scratch_shapes=[pltpu.VMEM((1, LANES), F32), pltpu.VMEM((1, LANES), F32), pltpu.VMEM((1, LANES), F32)],
        compiler_params=_cparams(("arbitrary", "arbitrary")),
        name="moe_route",
    )(logits)


def _row_copy(src, si, dst, di, sem):
    return pltpu.make_async_copy(src.at[pl.ds(si, 1), :], dst.at[pl.ds(di, 1), :], sem)


def _dispatch_kernel(cnt_ref, pst_ref, nu_ref, pos_ref, x_ref, xs_ref, zero_scr, sem, *, n_experts, tm, tt):
    i = pl.program_id(0)
    p_rows = xs_ref.shape[0]

    @pl.when(i == 0)
    def _():
        zero_scr[...] = jnp.zeros_like(zero_scr)
        zrows = zero_scr.shape[0]

        def pad_copy(e):
            start = jnp.minimum((pst_ref[e] + cnt_ref[e]) // SUBLANES * SUBLANES, p_rows - zrows)
            return pltpu.make_async_copy(zero_scr, xs_ref.at[pl.ds(pl.multiple_of(start, SUBLANES), zrows), :], sem)

        def start_pad(e, c):
            pad_copy(e).start()
            return c

        def wait_pad(e, c):
            pad_copy(e).wait()
            return c

        lax.fori_loop(0, n_experts, start_pad, 0)
        lax.fori_loop(0, n_experts, wait_pad, 0)

        def zero_tail(blk, c):
            cp = pltpu.make_async_copy(zero_scr.at[pl.ds(0, tm), :],
                                       xs_ref.at[pl.ds(pl.multiple_of(blk * tm, tm), tm), :], sem)
            cp.start()
            cp.wait()
            return c

        lax.fori_loop(nu_ref[0], p_rows // tm, zero_tail, 0)

    def start_rows(r8, c):
        for rr in range(SUBLANES):
            for k in range(TOP_K):
                p = pos_ref[(r8 * SUBLANES + rr) * TOP_K + k]
                _row_copy(x_ref.at[r8], rr, xs_ref, p, sem).start()
        return c

    def wait_rows(r, c):
        for k in range(TOP_K):
            _row_copy(x_ref.at[0], 0, xs_ref, 0, sem).wait()
        return c

    lax.fori_loop(0, tt // SUBLANES, start_rows, 0)
    lax.fori_loop(0, tt, wait_rows, 0, unroll=2)


def moe_dispatch(xpk, pos_flat, cnt, pst, n_used, p_rows, *, n_experts, tm, tt):
    t, dh = xpk.shape
    kern = functools.partial(_dispatch_kernel, n_experts=n_experts, tm=tm, tt=tt)
    return pl.pallas_call(
        kern,
        grid_spec=pltpu.PrefetchScalarGridSpec(
            num_scalar_prefetch=3,
            grid=(t // tt,),
            in_specs=[pl.BlockSpec((tt * TOP_K,), lambda i, *_: (i,), memory_space=pltpu.SMEM),
                      pl.BlockSpec((tt // SUBLANES, SUBLANES, dh), lambda i, *_: (i, 0, 0))],
            out_specs=pl.BlockSpec(memory_space=pl.ANY),
            scratch_shapes=[pltpu.VMEM((tm + SUBLANES, dh), U32), pltpu.SemaphoreType.DMA(())],
        ),
        out_shape=jax.ShapeDtypeStruct((p_rows, dh), U32),
        compiler_params=_cparams(("arbitrary",)),
        name="moe_dispatch",
    )(cnt, pst, n_used, pos_flat, xpk.reshape(t // SUBLANES, SUBLANES, dh))


def _new_expert(be_ref, i):
    prev = be_ref[jnp.maximum(i - 1, 0)]
    return (i == 0) | (be_ref[i] != prev)


def _by_block_rows(rows, tm, compute, out_ref):
    half = tm // 2

    def part():
        compute(half)
        out_ref[half:, :] = jnp.zeros((tm - half, out_ref.shape[1]), out_ref.dtype)

    pl.when(rows > half)(functools.partial(compute, tm))
    pl.when((rows > 0) & (rows <= half))(part)

    @pl.when(rows == 0)
    def _():
        out_ref[...] = jnp.zeros_like(out_ref)


def _moe_gu_kernel(be_ref, br_ref, x_ref, wg_ref, wu_ref, bg_ref, bu_ref, a_ref, wgb, wub):
    i = pl.program_id(1)
    tm, hw = x_ref.shape

    @pl.when(_new_expert(be_ref, i))
    def _():
        wgb[...] = wg_ref[...].astype(BF16)
        wub[...] = wu_ref[...].astype(BF16)

    def compute(n):
        lo, hi = _unpack_halves(x_ref[:n, :])
        gate = _dot(lo, wgb[:hw, :]) + _dot(hi, wgb[hw:, :]) + bg_ref[...]
        up = _dot(lo, wub[:hw, :]) + _dot(hi, wub[hw:, :]) + bu_ref[...]
        gate = jnp.minimum(gate, SWIGLU_LIMIT)
        up = jnp.clip(up, -SWIGLU_LIMIT, SWIGLU_LIMIT)
        sig = 1.0 / (1.0 + jnp.exp(-SWIGLU_ALPHA * gate))
        a_ref[:n, :] = (gate * sig * (up + 1.0)).astype(a_ref.dtype)

    _by_block_rows(br_ref[i], tm, compute, a_ref)


def _used_block(i, br):
    return jnp.where(br[i] > 0, i, 0)


def moe_gate_up(xs, w_gu, b_gu, layer, blk_e, blk_rows, *, tm, tn):
    p, hw = xs.shape
    d = 2 * hw
    f = w_gu.shape[3] // 2
    assert f % tn == 0 and p % tm == 0
    nj = f // tn
    wspec = lambda off: pl.BlockSpec((None, None, d, tn), lambda j, i, be, nu: (layer, be[i], 0, off + j))
    bspec = lambda off: pl.BlockSpec((None, None, 1, tn), lambda j, i, be, nu: (layer, be[i], 0, off + j))
    return pl.pallas_call(
        _moe_gu_kernel,
        grid_spec=pltpu.PrefetchScalarGridSpec(
            num_scalar_prefetch=2,
            grid=(nj, p // tm),
            in_specs=[pl.BlockSpec((tm, hw), lambda j, i, be, nu: (_used_block(i, nu), 0)),
                      wspec(0), wspec(nj), bspec(0), bspec(nj)],
            out_specs=pl.BlockSpec((tm, tn), lambda j, i, be, nu: (i, j)),
            scratch_shapes=[pltpu.VMEM((d, tn), BF16), pltpu.VMEM((d, tn), BF16)],
        ),
        out_shape=jax.ShapeDtypeStruct((p, f), BF16),
        compiler_params=_cparams(("arbitrary", "arbitrary")),
        name="moe_gate_up",
    )(blk_e, blk_rows, xs, w_gu, w_gu, b_gu, b_gu)


def _moe_down_kernel(be_ref, br_ref, a_ref, w_ref, b_ref, y_ref, wb):
    i = pl.program_id(1)

    @pl.when(_new_expert(be_ref, i))
    def _():
        wb[...] = w_ref[...].astype(BF16)

    def compute(n):
        y_ref[:n, :] = _dot(a_ref[:n, :], wb[...]) + b_ref[...]

    _by_block_rows(br_ref[i], a_ref.shape[0], compute, y_ref)


def moe_down(a, w_down, b_down, layer, blk_e, blk_rows, *, tm, tn):
    p, f = a.shape
    d = w_down.shape[3]
    assert d % tn == 0 and p % tm == 0
    return pl.pallas_call(
        _moe_down_kernel,
        grid_spec=pltpu.PrefetchScalarGridSpec(
            num_scalar_prefetch=2,
            grid=(d // tn, p // tm),
            in_specs=[pl.BlockSpec((tm, f), lambda j, i, be, nu: (i, 0)),
                      pl.BlockSpec((None, None, f, tn), lambda j, i, be, nu: (layer, be[i], 0, j)),
                      pl.BlockSpec((None, None, 1, tn), lambda j, i, be, nu: (layer, be[i], 0, j))],
            out_specs=pl.BlockSpec((tm, tn), lambda j, i, be, nu: (i, j)),
            scratch_shapes=[pltpu.VMEM((f, tn), BF16)],
        ),
        out_shape=jax.ShapeDtypeStruct((p, d), F32),
        compiler_params=_cparams(("arbitrary", "arbitrary")),
        name="moe_down",
    )(blk_e, blk_rows, a, w_down, b_down)


def _combine_kernel(pos_ref, posn_ref, gate_ref, x_ref, g_ref, b_ref, ys_ref, o_ref, ob_ref, buf, sem,
                    *, alpha, tt):
    i = pl.program_id(0)
    n = pl.num_programs(0)
    slot = i % 2

    def start_tile(p_ref, s):
        def body(r8, c):
            for rr in range(SUBLANES):
                for k in range(TOP_K):
                    p = p_ref[(r8 * SUBLANES + rr) * TOP_K + k]
                    _row_copy(ys_ref, p, buf.at[s, k, r8], rr, sem.at[s]).start()
            return c
        lax.fori_loop(0, tt // SUBLANES, body, 0)

    @pl.when(i == 0)
    def _():
        start_tile(pos_ref, 0)

    @pl.when(i + 1 < n)
    def _():
        start_tile(posn_ref, 1 - slot)

    def wait_rows(r, c):
        for k in range(TOP_K):
            _row_copy(ys_ref, 0, buf.at[slot, k, 0], 0, sem.at[slot]).wait()
        return c

    lax.fori_loop(0, tt, wait_rows, 0, unroll=2)
    gates = gate_ref[...]
    d = x_ref.shape[1]
    f = gates[:, 0:1] * buf[slot, 0].reshape(tt, d)
    for k in range(1, TOP_K):
        f = f + gates[:, k:k + 1] * buf[slot, k].reshape(tt, d)
    y = _layer_norm(alpha * x_ref[...] + f, g_ref[...], b_ref[...])
    o_ref[...] = y
    ob_ref[...] = y.astype(BF16)


def moe_combine_ln(ys, pos_flat, gates, x, g, b, *, alpha, tt):
    t, d = x.shape
    nt = t // tt
    tile = pl.BlockSpec((tt, d), lambda i: (i, 0))
    row = pl.BlockSpec((1, d), lambda i: (0, 0))
    kern = functools.partial(_combine_kernel, alpha=alpha, tt=tt)
    return pl.pallas_call(
        kern,
        grid=(nt,),
        in_specs=[pl.BlockSpec((tt * TOP_K,), lambda i: (i,), memory_space=pltpu.SMEM),
                  pl.BlockSpec((tt * TOP_K,), lambda i: (jnp.minimum(i + 1, nt - 1),), memory_space=pltpu.SMEM),
                  pl.BlockSpec((tt, LANES), lambda i: (i, 0)),
                  tile, row, row,
                  pl.BlockSpec(memory_space=pl.ANY)],
        out_specs=[tile, tile],
        out_shape=[jax.ShapeDtypeStruct((t, d), F32), jax.ShapeDtypeStruct((t, d), BF16)],
        scratch_shapes=[pltpu.VMEM((2, TOP_K, tt // SUBLANES, SUBLANES, d), F32),
                        pltpu.SemaphoreType.DMA((2,))],
        compiler_params=_cparams(("arbitrary",)),
        name="moe_combine_ln",
    )(pos_flat, pos_flat, gates, x, g, b, ys)


def kernel(x_prompt, x_sample, mem_prompt, mem_sample, ev_w_in, ev_w_out, ev_sink, ev_lam_q1, ev_lam_k1, ev_lam_q2, ev_lam_k2, ev_dnorm_w, od_w_in, od_w_out, od_conv_w, mem_wq, mem_wkv, mem_wo, ln_g, ln_b, moe_router_w, moe_router_b, moe_w_gu, moe_b_gu, moe_w_down, moe_b_down):
    b1, s1, d = x_prompt.shape
    b2, s2, _ = x_sample.shape
    segs = Segs(b1, s1, b2, s2)
    t = segs.t
    depth = ln_g.shape[0]
    n_mem = mem_prompt.shape[1]
    n_experts = moe_router_w.shape[2]
    alpha = (2 * depth) ** 0.25
    n_heads = d // (2 * HEAD_DIM)
    n_kv = n_heads // 4
    c_ch = d // 2
    dils = tuple(dil for _, dil in D_PATTERNS)

    x = jnp.concatenate([x_prompt.reshape(-1, d), x_sample.reshape(-1, d)], axis=0)
    xb = x.astype(BF16)
    memb = jnp.concatenate([mem_prompt.reshape(-1, d), mem_sample.reshape(-1, d)], axis=0).astype(BF16)
    tables = rope_tables(max(s1, s2))
    dtabs = dilated_tables(tables)
    ev_modes = jnp.array([MODE_ROPE128_Q] * n_heads + [MODE_ROPE128] * n_kv + [MODE_NONE] * n_kv
                         + [MODE_ROPE64_Q] * n_heads + [MODE_ROPE64] * n_heads
                         + [MODE_NONE] * n_heads, jnp.int32)
    od_modes = jnp.zeros((3 * c_ch // LANES,), jnp.int32)
    mem_modes = jnp.zeros((mem_wkv.shape[2] // LANES,), jnp.int32)
    mem_segs = Segs(b1, n_mem, b2, n_mem)
    rw_pad = jnp.pad(moe_router_w, ((0, 0), (0, 0), (0, LANES - n_experts)))
    rb_pad = jnp.pad(moe_router_b, ((0, 0), (0, LANES - n_experts)))[:, None, :]
    b_gu4 = moe_b_gu[:, :, None, :]
    b_down4 = moe_b_down[:, :, None, :]
    moe_tm = 512
    n_blocks = -(-(t * TOP_K + n_experts * (moe_tm - 1)) // moe_tm)
    p_rows = n_blocks * moe_tm

    for l in range(depth):
        j = l // 2
        if l % 2 == 0:
            lambda_init = 0.8 - 0.6 * math.exp(-0.3 * l)
            w_f = ev_w_in[j]
            kdim = w_f.shape[0]
            nb_lo, nb_hi = (n_heads + 2 * n_kv) * LANES, (3 * n_heads + 2 * n_kv) * LANES
            w_bqk = w_f[:, nb_lo:nb_hi].reshape(kdim, 2 * n_heads, 2, 2, B_DIM // 2)
            w_bqk = w_bqk.swapaxes(2, 3).reshape(kdim, nb_hi - nb_lo)
            w_in = jnp.concatenate([w_f[:, :nb_lo], w_bqk, w_f[:, nb_hi:]], axis=1).astype(BF16)
            n_in = w_in.shape[1]
            tn = n_in // 6 if (n_in // 6) % LANES == 0 else n_in // 18
            h = proj(xb, w_in, ev_modes, tables, segs, col0=0, ncols=n_in, tm=512, tn=tn,
                     out_dtype=BF16)
            lamv = jnp.stack([ev_lam_q1[j], ev_lam_k1[j], ev_lam_q2[j], ev_lam_k2[j]])
            oa, ob = [], []
            for row0, bsz, s in segs.groups():
                oa.append(window_attention(h, ev_sink[j], row0, bsz, s, d_model=d))
                ob.append(diff_attention(h, lamv, ev_dnorm_w[j][None, :], row0, bsz, s, lambda_init,
                                         d_model=d, tq=1024, tk=2048, n_streams=4))
            fs = [jnp.concatenate(oa, axis=0), jnp.concatenate(ob, axis=0)]
            w_out = ev_w_out[j].astype(BF16)
        else:
            w_in = od_w_in[j].astype(BF16)
            hc = proj(xb, w_in, od_modes, tables, segs, col0=0, ncols=3 * c_ch, tm=512,
                      tn=c_ch, out_dtype=F32)
            hqs = [proj_dilated(xb, w_in, dtabs, segs, g, dil, col0=3 * c_ch, tm=512)
                   for g, dil in enumerate(dils)]
            oc, od = [], []
            for row0, bsz, s in segs.groups():
                oc.append(gated_conv(hc, od_conv_w[j], row0, bsz, s, ts=512, tc=c_ch // 2))
                outs = [dilated_group(hq, dil, row0, bsz, s) for hq, dil in zip(hqs, dils)]
                od.append(dilated_combine([o for o, _ in outs], [ls for _, ls in outs], dils, tm=512))
            fs = [jnp.concatenate(oc, axis=0), jnp.concatenate(od, axis=0)]
            w_out = od_w_out[j].astype(BF16)
        x, xb = outproj_ln(fs, w_out, x, ln_g[l, 0][None, :], ln_b[l, 0][None, :], alpha=alpha, tm=256)

        kv = proj(memb, mem_wkv[l].astype(BF16), mem_modes, tables, mem_segs, col0=0,
                  ncols=mem_wkv.shape[2], tm=n_mem, tn=mem_wkv.shape[2] // 2, out_dtype=BF16)
        x, xpk, logits = mem_attention_ln(xb, x, kv, mem_wq[l].astype(BF16), mem_wo[l].astype(BF16),
                                          ln_g[l, 1][None, :], ln_b[l, 1][None, :], rw_pad[l], rb_pad[l],
                                          segs, alpha=alpha, tm=512, n_mem=n_mem)

        pos, gates, meta = route(logits, n_experts, tm=moe_tm, tt=512)
        pos_flat = pos[:, :TOP_K].reshape(-1)
        cnt, pst, end_blk = meta[0, :n_experts], meta[1, :n_experts], meta[2, :n_experts]
        blk_e = jnp.minimum(jnp.sum(end_blk[None, :] <= jnp.arange(n_blocks, dtype=jnp.int32)[:, None], axis=1),
                            n_experts - 1).astype(jnp.int32)
        n_used = end_blk[n_experts - 1:]
        blk_lo = (jnp.arange(n_blocks, dtype=jnp.int32) * moe_tm)[:, None]
        blk_rows = jnp.sum(jnp.clip(jnp.minimum((pst + cnt)[None, :], blk_lo + moe_tm)
                                    - jnp.maximum(pst[None, :], blk_lo), 0, moe_tm), axis=1).astype(jnp.int32)
        xs = moe_dispatch(xpk, pos_flat, cnt, pst, n_used, p_rows, n_experts=n_experts, tm=moe_tm, tt=256)
        a = moe_gate_up(xs, moe_w_gu, b_gu4, l, blk_e, blk_rows, tm=moe_tm, tn=1024)
        ys = moe_down(a, moe_w_down, b_down4, l, blk_e, blk_rows, tm=moe_tm, tn=d)
        x, xb = moe_combine_ln(ys, pos_flat, gates, x, ln_g[l, 2][None, :], ln_b[l, 2][None, :],
                               alpha=alpha, tt=128)

    y1 = x[:segs.t1].reshape(b1, s1, d)
    y2 = x[segs.t1:].reshape(b2, s2, d)
    return (y1, y2)
```

```python
import functools
import math

import jax
import jax.numpy as jnp
from jax import lax
from jax.experimental import pallas as pl
from jax.experimental.pallas import tpu as pltpu

F32 = jnp.float32
BF16 = jnp.bfloat16
U32 = jnp.uint32
LANES = 128
SUBLANES = 8
VMEM_LIMIT = 56 * 1024 * 1024

HEAD_DIM = 128
ROPE_THETA = 10000.0
A_WINDOW = 128
B_DIM = 64
C_WIDTH = 3
D_PATTERNS = ((128, 1), (512, 4), (2048, 16))
D_HEADS_PER_GROUP = 4
D_NK = 64
MEM_HEADS = 4
TOP_K = 4
SWIGLU_LIMIT = 7.0
SWIGLU_ALPHA = 1.702
LN_EPS = 1e-5
NEG_INF = -1e30

MODE_NONE, MODE_ROPE128, MODE_ROPE128_Q, MODE_ROPE64, MODE_ROPE64_Q = 0, 1, 2, 3, 4


def _cparams(sem):
    return pltpu.CompilerParams(dimension_semantics=sem, vmem_limit_bytes=VMEM_LIMIT)


def _dot(a, b):
    return jnp.dot(a, b, preferred_element_type=F32)


def _dot_t(a, b):
    return lax.dot_general(a, b, (((1,), (1,)), ((), ())), preferred_element_type=F32)


class Segs:
    def __init__(self, b1, s1, b2, s2):
        self.b1, self.s1, self.b2, self.s2 = b1, s1, b2, s2
        self.t1 = b1 * s1
        self.t = b1 * s1 + b2 * s2

    def groups(self):
        return ((0, self.b1, self.s1), (self.t1, self.b2, self.s2))

    def pos_tile(self, i, tm):
        n1 = self.t1 // tm
        return jnp.where(i < n1, i % (self.s1 // tm), (i - n1) % (self.s2 // tm))

    def batch_of_tile(self, i, tm):
        n1 = self.t1 // tm
        return jnp.where(i < n1, (i * tm) // self.s1, self.b1 + ((i - n1) * tm) // self.s2)


def _layer_norm(z, g, b):
    mu = jnp.mean(z, axis=-1, keepdims=True)
    zc = z - mu
    var = jnp.mean(zc * zc, axis=-1, keepdims=True)
    return zc * lax.rsqrt(var + LN_EPS) * g + b


def _rope128(a, c, s):
    return a * c + pltpu.roll(a, 64, 1) * s


def _proj_kernel(mode_ref, x_ref, w_ref, c128_ref, s128_ref, c64_ref, s64_ref, o_ref,
                 *, n_slabs, slab0, q128_scale, q64_scale):
    j = pl.program_id(0)
    acc = _dot(x_ref[...], w_ref[...])
    for s in range(n_slabs):
        a = acc[:, s * LANES:(s + 1) * LANES]
        mode = mode_ref[slab0 + j * n_slabs + s]
        w0 = jnp.where(mode == MODE_NONE, 1.0, 0.0)
        w128 = jnp.where(mode == MODE_ROPE128, 1.0, jnp.where(mode == MODE_ROPE128_Q, q128_scale, 0.0))
        w64 = jnp.where(mode == MODE_ROPE64, 1.0, jnp.where(mode == MODE_ROPE64_Q, q64_scale, 0.0))
        c = w0 + w128 * c128_ref[...] + w64 * c64_ref[...]
        sn = w128 * s128_ref[...] + w64 * s64_ref[...]
        o_ref[:, s * LANES:(s + 1) * LANES] = _rope128(a, c, sn).astype(o_ref.dtype)


def proj(xb, w, modes, tables, segs, *, col0, ncols, tm, tn, out_dtype):
    m, k = xb.shape
    assert ncols % tn == 0 and col0 % tn == 0 and m % tm == 0 and tn % LANES == 0
    n_slabs = tn // LANES
    jb = col0 // tn
    tab_spec = pl.BlockSpec((tm, LANES), lambda j, i, *_: (segs.pos_tile(i, tm), 0))
    kern = functools.partial(_proj_kernel, n_slabs=n_slabs, slab0=col0 // LANES,
                             q128_scale=HEAD_DIM ** -0.5, q64_scale=B_DIM ** -0.5)
    return pl.pallas_call(
        kern,
        grid_spec=pltpu.PrefetchScalarGridSpec(
            num_scalar_prefetch=1,
            grid=(ncols // tn, m // tm),
            in_specs=[pl.BlockSpec((tm, k), lambda j, i, *_: (i, 0)),
                      pl.BlockSpec((k, tn), lambda j, i, *_: (0, jb + j)),
                      tab_spec, tab_spec, tab_spec, tab_spec],
            out_specs=pl.BlockSpec((tm, tn), lambda j, i, *_: (i, j)),
        ),
        out_shape=jax.ShapeDtypeStruct((m, ncols), out_dtype),
        compiler_params=_cparams(("arbitrary", "arbitrary")),
        name="proj",
    )(modes, xb, w, *tables)


def rope_tables(s):
    inv = ROPE_THETA ** (-jnp.arange(0, HEAD_DIM, 2, dtype=F32) / HEAD_DIM)
    ang = jnp.arange(s, dtype=F32)[:, None] * inv[None, :]
    c, sn = jnp.cos(ang), jnp.sin(ang)
    c128 = jnp.concatenate([c, c], axis=1)
    s128 = jnp.concatenate([-sn, sn], axis=1)
    invb = ROPE_THETA ** (-jnp.arange(0, B_DIM, 2, dtype=F32) / B_DIM)
    angb = jnp.arange(s, dtype=F32)[:, None] * invb[None, :]
    cb, sb = jnp.cos(angb), jnp.sin(angb)
    c64 = jnp.concatenate([cb, cb, cb, cb], axis=1)
    s64 = jnp.concatenate([-sb, -sb, sb, sb], axis=1)
    return (c128, s128, c64, s64)


B_SLAB_PERM = tuple(list(range(0, 32)) + list(range(64, 96)) + list(range(32, 64)) + list(range(96, 128)))


def _is_comp0(lane):
    return (lane // (B_DIM // 2)) % 2 == 0


def _window_kernel(sink_ref, q_ref, kp_ref, kc_ref, kn_ref, vp_ref, vc_ref, vn_ref, o_ref,
                   *, n_heads, n_kv, blk):
    n = pl.program_id(1)
    nb = pl.num_programs(1)
    r = lax.broadcasted_iota(jnp.int32, (blk, 3 * blk), 0)
    c = lax.broadcasted_iota(jnp.int32, (blk, 3 * blk), 1)
    valid = jnp.abs(r + blk - c) <= A_WINDOW
    valid &= (c >= blk) | (n > 0)
    valid &= (c < 2 * blk) | (n < nb - 1)
    g = n_heads // n_kv
    for kv in range(n_kv):
        cs = slice(kv * HEAD_DIM, (kv + 1) * HEAD_DIM)
        k3 = jnp.concatenate([kp_ref[:, cs], kc_ref[:, cs], kn_ref[:, cs]], axis=0)
        v3 = jnp.concatenate([vp_ref[:, cs], vc_ref[:, cs], vn_ref[:, cs]], axis=0)
        for gi in range(g):
            h = kv * g + gi
            hs = slice(h * HEAD_DIM, (h + 1) * HEAD_DIM)
            sc = jnp.where(valid, _dot_t(q_ref[:, hs], k3), NEG_INF)
            sk = sink_ref[h]
            m = jnp.maximum(jnp.max(sc, axis=-1, keepdims=True), sk)
            p = jnp.exp(sc - m)
            denom = jnp.sum(p, axis=-1, keepdims=True) + jnp.exp(sk - m)
            o = _dot(p.astype(BF16), v3) / denom
            o_ref[:, hs] = o.astype(o_ref.dtype)


def window_attention(h, sink, row0, bsz, s, *, d_model):
    blk = A_WINDOW
    n_heads = d_model // (2 * HEAD_DIM)
    n_kv = n_heads // 4
    qw, kw = n_heads * HEAD_DIM, n_kv * HEAD_DIM
    nb = s // blk
    rb0 = row0 // blk
    kcol, vcol = qw // kw, qw // kw + 1

    def rows(b, n):
        return rb0 + b * nb + n

    def spec(col, shift):
        def imap(b, n):
            nn = jnp.clip(n + shift, 0, nb - 1)
            return (rows(b, nn), col)
        return pl.BlockSpec((blk, kw), imap)

    kern = functools.partial(_window_kernel, n_heads=n_heads, n_kv=n_kv, blk=blk)
    return pl.pallas_call(
        kern,
        grid=(bsz, nb),
        in_specs=[pl.BlockSpec(memory_space=pltpu.SMEM),
                  pl.BlockSpec((blk, qw), lambda b, n: (rows(b, n), 0)),
                  spec(kcol, -1), spec(kcol, 0), spec(kcol, 1),
                  spec(vcol, -1), spec(vcol, 0), spec(vcol, 1)],
        out_specs=pl.BlockSpec((blk, qw), lambda b, n: (b * nb + n, 0)),
        out_shape=jax.ShapeDtypeStruct((bsz * s, qw), BF16),
        compiler_params=_cparams(("arbitrary", "arbitrary")),
        name="window_attn",
    )(sink, h, h, h, h, h, h, h)


def _diff_kernel(lamv_ref, nw_ref, q_ref, k_ref, v_ref, o_ref, *, tq, tk, n_streams, lambda_init):
    s = k_ref.shape[0]
    ts = tq // n_streams
    lane = lax.broadcasted_iota(jnp.int32, (ts, LANES), 1)
    q2s = []
    for st in range(n_streams):
        q = q_ref[st * ts:(st + 1) * ts, :]
        zero = jnp.zeros_like(q)
        q2s.append(jnp.concatenate([jnp.where(_is_comp0(lane), q, zero),
                                    jnp.where(_is_comp0(lane), zero, q)], axis=0))

    def body(i, carry):
        ks = pl.multiple_of(i * tk, tk)
        kb = k_ref[pl.ds(ks, tk), :]
        vb = v_ref[pl.ds(ks, tk), :]
        out = []
        for st in range(n_streams):
            m, l, acc = carry[st]
            sc = _dot_t(q2s[st], kb)
            m_new = jnp.maximum(m, jnp.max(sc, axis=-1, keepdims=True))
            a = jnp.exp(m - m_new)
            p = jnp.exp(sc - m_new)
            l = a * l + jnp.sum(p, axis=-1, keepdims=True)
            acc = a * acc + _dot(p.astype(BF16), vb)
            out.append((m_new, l, acc))
        return tuple(out)

    init = tuple((jnp.full((2 * ts, 1), NEG_INF, F32), jnp.zeros((2 * ts, 1), F32),
                  jnp.zeros((2 * ts, LANES), F32)) for _ in range(n_streams))
    res = lax.fori_loop(0, s // tk, body, init)
    lv = lamv_ref[...]
    lam = (jnp.exp(jnp.sum(lv[0:1] * lv[1:2], axis=-1, keepdims=True))
           - jnp.exp(jnp.sum(lv[2:3] * lv[3:4], axis=-1, keepdims=True)) + lambda_init)
    for st in range(n_streams):
        m, l, acc = res[st]
        on = acc / l
        of = on[:ts] - lam * on[ts:]
        of = of * lax.rsqrt(jnp.mean(of * of, axis=-1, keepdims=True) + LN_EPS) * nw_ref[...]
        o_ref[st * ts:(st + 1) * ts, :] = (of * (1.0 - lambda_init)).astype(o_ref.dtype)


def diff_attention(h, lamv, norm_w, row0, bsz, s, lambda_init, *, d_model, tq, tk, n_streams):
    n_heads = d_model // (2 * HEAD_DIM)
    a_cols = (n_heads + 2 * (n_heads // 4))
    qc, kc, vc = a_cols, a_cols + n_heads, a_cols + 2 * n_heads
    nq = s // tq
    assert row0 % s == 0
    sb0 = row0 // s
    qb0 = row0 // tq
    kern = functools.partial(_diff_kernel, tq=tq, tk=tk, n_streams=n_streams, lambda_init=lambda_init)
    return pl.pallas_call(
        kern,
        grid=(bsz, n_heads, nq),
        in_specs=[pl.BlockSpec((4, B_DIM), lambda b, hh, n: (0, 0)),
                  pl.BlockSpec((1, LANES), lambda b, hh, n: (0, 0)),
                  pl.BlockSpec((tq, LANES), lambda b, hh, n: (qb0 + b * nq + n, qc + hh)),
                  pl.BlockSpec((s, LANES), lambda b, hh, n: (sb0 + b, kc + hh)),
                  pl.BlockSpec((s, LANES), lambda b, hh, n: (sb0 + b, vc + hh))],
        out_specs=pl.BlockSpec((tq, LANES), lambda b, hh, n: (b * nq + n, hh)),
        out_shape=jax.ShapeDtypeStruct((bsz * s, n_heads * LANES), BF16),
        compiler_params=_cparams(("arbitrary", "arbitrary", "arbitrary")),
        name="diff_attn",
    )(lamv, norm_w, h, h, h)


def _conv_kernel(cw_ref, b_ref, c_ref, x_ref, cp_ref, xp_ref, cn_ref, xn_ref, o_ref, *, ts):
    n = pl.program_id(1)
    nt = pl.num_programs(1)
    u = c_ref[...] * x_ref[...]
    prev = jnp.where(n > 0, cp_ref[7:8, :] * xp_ref[7:8, :], 0.0)
    nxt = jnp.where(n < nt - 1, cn_ref[0:1, :] * xn_ref[0:1, :], 0.0)
    row = lax.broadcasted_iota(jnp.int32, u.shape, 0)
    um1 = jnp.where(row == 0, prev, pltpu.roll(u, 1, 0))
    up1 = jnp.where(row == ts - 1, nxt, pltpu.roll(u, ts - 1, 0))
    conv = um1 * cw_ref[0:1, :] + u * cw_ref[1:2, :] + up1 * cw_ref[2:3, :]
    o_ref[...] = (b_ref[...] * conv).astype(o_ref.dtype)


def gated_conv(hc, conv_w, row0, bsz, s, *, ts, tc):
    c = hc.shape[1] // 3
    nt = s // ts
    nc = c // tc
    rb0 = row0 // ts
    r8 = ts // 8

    def main(col):
        return pl.BlockSpec((ts, tc), lambda b, n, j: (rb0 + b * nt + n, col * nc + j))

    def halo(col, shift):
        def imap(b, n, j):
            blk8 = (rb0 + b * nt + n) * r8 + (-1 if shift < 0 else r8)
            return (jnp.clip(blk8, 0, hc.shape[0] // 8 - 1), col * nc + j)
        return pl.BlockSpec((8, tc), imap)

    kern = functools.partial(_conv_kernel, ts=ts)
    return pl.pallas_call(
        kern,
        grid=(bsz, nt, nc),
        in_specs=[pl.BlockSpec((C_WIDTH, tc), lambda b, n, j: (0, j)),
                  main(0), main(1), main(2), halo(1, -1), halo(2, -1), halo(1, 1), halo(2, 1)],
        out_specs=pl.BlockSpec((ts, tc), lambda b, n, j: (b * nt + n, j)),
        out_shape=jax.ShapeDtypeStruct((bsz * s, c), BF16),
        compiler_params=_cparams(("arbitrary", "arbitrary", "arbitrary")),
        name="gated_conv",
    )(conv_w, hc, hc, hc, hc, hc, hc, hc)


def _proj_dil_kernel(x_ref, w_ref, c_ref, s_ref, o_ref, *scrs, dil, tm):
    rows = tm // dil
    acc = _dot(x_ref[...], w_ref[...])
    for sl, scr in enumerate(scrs):
        cols = slice(sl * LANES, (sl + 1) * LANES)
        a = _rope128(acc[:, cols], c_ref[...], s_ref[...])
        if dil == 1:
            o_ref[0, :, cols] = a.astype(o_ref.dtype)
        else:
            scr[...] = a
    if dil > 1:
        for r in range(dil):
            for sl, scr in enumerate(scrs):
                o_ref[r, :, sl * LANES:(sl + 1) * LANES] = scr[pl.ds(r, rows, stride=dil), :].astype(o_ref.dtype)


def dilated_tables(tables):
    c128, s128 = tables[0], tables[1]
    q_scale = HEAD_DIM ** -0.5
    return (jnp.stack([c128 * q_scale, c128, jnp.ones_like(c128)]),
            jnp.stack([s128 * q_scale, s128, jnp.zeros_like(s128)]))


def proj_dilated(xb, w, dtabs, segs, g, dil, *, col0, tm):
    m, k = xb.shape
    gw = D_HEADS_PER_GROUP * HEAD_DIM
    n_groups = len(D_PATTERNS)
    cb0 = col0 // gw
    tab = pl.BlockSpec((None, tm, LANES), lambda j, i: (j, segs.pos_tile(i, tm), 0))
    kern = functools.partial(_proj_dil_kernel, dil=dil, tm=tm)
    return pl.pallas_call(
        kern,
        grid=(3, m // tm),
        in_specs=[pl.BlockSpec((tm, k), lambda j, i: (i, 0)),
                  pl.BlockSpec((k, gw), lambda j, i: (0, cb0 + j * n_groups + g)),
                  tab, tab],
        out_specs=pl.BlockSpec((dil, tm // dil, gw), lambda j, i: (0, i, j)),
        out_shape=jax.ShapeDtypeStruct((dil, m // dil, 3 * gw), BF16),
        scratch_shapes=[pltpu.VMEM((tm, LANES), F32) for _ in range(gw // LANES)],
        compiler_params=_cparams(("arbitrary", "arbitrary")),
        name="proj_dilated",
    )(xb, w, dtabs[0], dtabs[1])


def _dilated_kernel(q_ref, kp_ref, kc_ref, kn_ref, vp_ref, vc_ref, vn_ref, o_ref, lse_ref,
                    *, blk, half):
    n = pl.program_id(2)
    nt = pl.num_programs(2)
    wk = blk + 2 * half
    r = lax.broadcasted_iota(jnp.int32, (blk, wk), 0)
    c = lax.broadcasted_iota(jnp.int32, (blk, wk), 1)
    valid = (c >= r) & (c <= r + 2 * half)
    valid &= (c >= half) | (n > 0)
    valid &= (c < half + blk) | (n < nt - 1)
    lane = lax.broadcasted_iota(jnp.int32, (blk, LANES), 1)
    lse_all = jnp.zeros((blk, LANES), F32)
    per = LANES // D_HEADS_PER_GROUP
    for hh in range(D_HEADS_PER_GROUP):
        hs = slice(hh * HEAD_DIM, (hh + 1) * HEAD_DIM)
        kk = jnp.concatenate([kp_ref[:, hs], kc_ref[:, hs], kn_ref[:, hs]], axis=0)
        vv = jnp.concatenate([vp_ref[:, hs], vc_ref[:, hs], vn_ref[:, hs]], axis=0)
        sc = jnp.where(valid, _dot_t(q_ref[:, hs], kk), NEG_INF)
        m = jnp.max(sc, axis=-1, keepdims=True)
        p = jnp.exp(sc - m)
        l = jnp.sum(p, axis=-1, keepdims=True)
        o_ref[:, hs] = _dot(p.astype(BF16), vv) / l
        lse_all = jnp.where(lane // per == hh, m + jnp.log(l), lse_all)
    lse_ref[...] = lse_all


def dilated_group(hq, dil, row0, bsz, s):
    blk, half = 2 * D_NK, D_NK
    gw = D_HEADS_PER_GROUP * HEAD_DIM
    sd = s // dil
    nt = sd // blk
    assert (row0 // dil) % blk == 0 and sd % blk == 0
    rb0 = row0 // dil // blk
    hb = blk // half

    def cur(sec):
        return pl.BlockSpec((None, blk, gw), lambda b, r, n: (r, rb0 + b * nt + n, sec))

    def halo(sec, shift):
        def imap(b, r, n):
            nn = jnp.clip(n * hb + (-1 if shift < 0 else hb), 0, nt * hb - 1)
            return (r, (rb0 + b * nt) * hb + nn, sec)
        return pl.BlockSpec((None, half, gw), imap)

    kern = functools.partial(_dilated_kernel, blk=blk, half=half)
    return pl.pallas_call(
        kern,
        grid=(bsz, dil, nt),
        in_specs=[cur(0), halo(1, -1), cur(1), halo(1, 1), halo(2, -1), cur(2), halo(2, 1)],
        out_specs=[pl.BlockSpec((None, blk, gw), lambda b, r, n: (r, b * nt + n, 0)),
                   pl.BlockSpec((None, blk, LANES), lambda b, r, n: (r, b * nt + n, 0))],
        out_shape=[jax.ShapeDtypeStruct((dil, bsz * sd, gw), F32),
                   jax.ShapeDtypeStruct((dil, bsz * sd, LANES), F32)],
        compiler_params=_cparams(("arbitrary", "arbitrary", "arbitrary")),
        name="dilated_attn",
    )(hq, hq, hq, hq, hq, hq, hq)


def _dil_combine_kernel(*refs, dils, tm):
    ng = len(dils)
    nh = D_HEADS_PER_GROUP
    o_refs, l_refs = refs[:ng], refs[ng:2 * ng]
    out_ref = refs[2 * ng]
    scr = refs[2 * ng + 1:]
    o_scr = [scr[g * (nh + 1):g * (nh + 1) + nh] for g in range(ng)]
    l_scr = [scr[g * (nh + 1) + nh] for g in range(ng)]
    for o_ref, l_ref, osc, lsc, dil in zip(o_refs, l_refs, o_scr, l_scr, dils):
        for r in range(dil):
            rs = pl.ds(r, tm // dil, stride=dil) if dil > 1 else pl.ds(0, tm)
            for hh in range(nh):
                osc[hh][rs, :] = o_ref[r, :, hh * HEAD_DIM:(hh + 1) * HEAD_DIM]
            lsc[rs, :] = l_ref[r]
    per = LANES // nh
    for hh in range(nh):
        hs = slice(hh * HEAD_DIM, (hh + 1) * HEAD_DIM)
        ls = slice(hh * per, hh * per + 1)
        lses = [lsc[:, ls] for lsc in l_scr]
        m = functools.reduce(jnp.maximum, lses)
        ws = [jnp.exp(a - m) for a in lses]
        den = functools.reduce(lambda a, b: a + b, ws)
        o = functools.reduce(lambda a, b: a + b, [(w / den) * osc[hh][...] for w, osc in zip(ws, o_scr)])
        out_ref[:, hs] = o.astype(out_ref.dtype)


def dilated_combine(os_, lses, dils, *, tm):
    gw = os_[0].shape[2]
    m = os_[0].shape[0] * os_[0].shape[1]
    ospecs = [pl.BlockSpec((d, tm // d, gw), lambda i: (0, i, 0)) for d in dils]
    lspecs = [pl.BlockSpec((d, tm // d, LANES), lambda i: (0, i, 0)) for d in dils]
    kern = functools.partial(_dil_combine_kernel, dils=dils, tm=tm)
    return pl.pallas_call(
        kern,
        grid=(m // tm,),
        in_specs=ospecs + lspecs,
        out_specs=pl.BlockSpec((tm, gw), lambda i: (i, 0)),
        out_shape=jax.ShapeDtypeStruct((m, gw), BF16),
        scratch_shapes=[pltpu.VMEM((tm, LANES), F32) for _ in range(len(dils) * (D_HEADS_PER_GROUP + 1))],
        compiler_params=_cparams(("arbitrary",)),
        name="dilated_combine",
    )(*os_, *lses)


def _outproj_kernel(*refs, n_in, alpha):
    f_refs = refs[:n_in]
    w_refs = refs[n_in:2 * n_in]
    x_ref, g_ref, b_ref, o_ref, ob_ref = refs[2 * n_in:]
    acc = alpha * x_ref[...]
    for f_ref, w_ref in zip(f_refs, w_refs):
        acc = acc + _dot(f_ref[...], w_ref[...])
    y = _layer_norm(acc, g_ref[...], b_ref[...])
    o_ref[...] = y
    ob_ref[...] = y.astype(BF16)


def outproj_ln(fs, w, x, g, b, *, alpha, tm):
    m, d = x.shape
    n_in = len(fs)
    widths = [f.shape[1] for f in fs]
    in_specs = [pl.BlockSpec((tm, wd), lambda i: (i, 0)) for wd in widths]
    ws = []
    off = 0
    for wd in widths:
        assert off % wd == 0
        in_specs.append(pl.BlockSpec((wd, d), functools.partial(lambda i, o: (o, 0), o=off // wd)))
        ws.append(w)
        off += wd
    row = pl.BlockSpec((1, d), lambda i: (0, 0))
    in_specs += [pl.BlockSpec((tm, d), lambda i: (i, 0)), row, row]
    kern = functools.partial(_outproj_kernel, n_in=n_in, alpha=alpha)
    return pl.pallas_call(
        kern,
        grid=(m // tm,),
        in_specs=in_specs,
        out_specs=[pl.BlockSpec((tm, d), lambda i: (i, 0)), pl.BlockSpec((tm, d), lambda i: (i, 0))],
        out_shape=[jax.ShapeDtypeStruct((m, d), F32), jax.ShapeDtypeStruct((m, d), BF16)],
        compiler_params=_cparams(("arbitrary",)),
        name="outproj_ln",
    )(*fs, *ws, x, g, b)


def _pack_halves(y):
    bits = lax.bitcast_convert_type(y.astype(BF16).astype(F32), U32)
    hw = y.shape[1] // 2
    return (bits[:, hw:] & jnp.uint32(0xFFFF0000)) | (bits[:, :hw] >> 16)


def _unpack_halves(pk):
    lo = lax.bitcast_convert_type(pk << 16, F32).astype(BF16)
    hi = lax.bitcast_convert_type(pk & jnp.uint32(0xFFFF0000), F32).astype(BF16)
    return lo, hi


def _mem_kernel(xb_ref, x_ref, kv_ref, wq_ref, wo_ref, g_ref, b_ref, rw_ref, rb_ref,
                o_ref, pk_ref, lg_ref, *, alpha):
    inner = wq_ref.shape[1]
    hd = inner // MEM_HEADS
    q = (_dot(xb_ref[...], wq_ref[...]) * hd ** -0.5).astype(BF16)
    outs = []
    for hh in range(MEM_HEADS):
        hs = slice(hh * hd, (hh + 1) * hd)
        sc = _dot_t(q[:, hs], kv_ref[:, hs])
        m = jnp.max(sc, axis=-1, keepdims=True)
        p = jnp.exp(sc - m)
        l = jnp.sum(p, axis=-1, keepdims=True)
        outs.append(_dot(p.astype(BF16), kv_ref[:, inner + hh * hd:inner + (hh + 1) * hd]) / l)
    o = jnp.concatenate(outs, axis=1).astype(BF16)
    y = _layer_norm(alpha * x_ref[...] + _dot(o, wo_ref[...]), g_ref[...], b_ref[...])
    o_ref[...] = y
    pk_ref[...] = _pack_halves(y)
    yb = y.astype(BF16)
    yl = (y - yb.astype(F32)).astype(BF16)
    rw = rw_ref[...]
    rwh = rw.astype(BF16)
    rwl = (rw - rwh.astype(F32)).astype(BF16)
    lg_ref[...] = _dot(yb, rwh) + _dot(yl, rwh) + _dot(yb, rwl) + rb_ref[...]


def mem_attention_ln(xb, x, kv, wq, wo, g, b, rw, rb, segs, *, alpha, tm, n_mem):
    m, d = x.shape
    inner = wq.shape[1]
    full = lambda a: pl.BlockSpec(a.shape, lambda i: (0, 0))
    tile = pl.BlockSpec((tm, d), lambda i: (i, 0))
    kern = functools.partial(_mem_kernel, alpha=alpha)
    return pl.pallas_call(
        kern,
        grid=(m // tm,),
        in_specs=[tile, tile,
                  pl.BlockSpec((n_mem, 2 * inner), lambda i: (segs.batch_of_tile(i, tm), 0)),
                  full(wq), full(wo), full(g), full(b), full(rw), full(rb)],
        out_specs=[tile, pl.BlockSpec((tm, d // 2), lambda i: (i, 0)),
                   pl.BlockSpec((tm, LANES), lambda i: (i, 0))],
        out_shape=[jax.ShapeDtypeStruct((m, d), F32), jax.ShapeDtypeStruct((m, d // 2), U32),
                   jax.ShapeDtypeStruct((m, LANES), F32)],
        compiler_params=_cparams(("arbitrary",)),
        name="mem_attn_ln",
    )(xb, x, kv, wq, wo, g, b, rw, rb)


def _route_kernel(lg_ref, pos_ref, gate_ref, meta_ref, cnt_ref, run_ref, pst_ref,
                  *, n_experts, tm, tt):
    ph = pl.program_id(0)
    i = pl.program_id(1)
    lane = lax.broadcasted_iota(jnp.int32, (tt, LANES), 1)
    lanef = lane.astype(F32)
    work = jnp.where(lane < n_experts, lg_ref[...], -jnp.inf)
    vals, hots = [], []
    for _ in range(TOP_K):
        m = jnp.max(work, axis=-1, keepdims=True)
        idx = jnp.min(jnp.where(work == m, lanef, float(LANES)), axis=-1, keepdims=True)
        hot = lanef == idx
        vals.append(m)
        hots.append(hot)
        work = jnp.where(hot, -jnp.inf, work)
    sel = functools.reduce(jnp.logical_or, hots).astype(F32)
    colsum = jnp.sum(sel, axis=0, keepdims=True)

    @pl.when((ph == 0) & (i == 0))
    def _():
        cnt_ref[...] = jnp.zeros_like(cnt_ref)

    @pl.when(ph == 0)
    def _():
        cnt_ref[...] += colsum

    @pl.when((ph == 1) & (i == 0))
    def _():
        cnt = cnt_ref[...]
        nblk = jnp.floor((cnt + (tm - 1)) / tm)
        a = lax.broadcasted_iota(jnp.int32, (LANES, LANES), 0)
        b = lax.broadcasted_iota(jnp.int32, (LANES, LANES), 1)
        upper = (a < b).astype(BF16)
        first = _dot(jnp.broadcast_to(nblk, (8, LANES)).astype(BF16), upper)[0:1]
        pst_ref[...] = first * tm
        run_ref[...] = jnp.zeros_like(run_ref)
        row = lax.broadcasted_iota(jnp.int32, meta_ref.shape, 0)
        meta = jnp.where(row == 0, cnt, jnp.where(row == 1, first * tm, jnp.where(row == 2, first + nblk, 0.0)))
        meta_ref[...] = meta.astype(jnp.int32)

    @pl.when(ph == 1)
    def _():
        r = lax.broadcasted_iota(jnp.int32, (tt, tt), 0)
        c = lax.broadcasted_iota(jnp.int32, (tt, tt), 1)
        before = _dot((r > c).astype(BF16), sel.astype(BF16)) + run_ref[...]
        posfull = pst_ref[...] + before
        e = [jnp.exp(v - vals[0]) for v in vals]
        den = functools.reduce(lambda x, y: x + y, e)
        pos = jnp.zeros((tt, LANES), F32)
        gat = jnp.zeros((tt, LANES), F32)
        for k in range(TOP_K):
            pk = jnp.sum(jnp.where(hots[k], posfull, 0.0), axis=-1, keepdims=True)
            pos = jnp.where(lane == k, pk, pos)
            gat = jnp.where(lane == k, e[k] / den, gat)
        pos_ref[...] = pos.astype(jnp.int32)
        gate_ref[...] = gat
        run_ref[...] += colsum


def route(logits, n_experts, *, tm, tt):
    t = logits.shape[0]
    tile = lambda: pl.BlockSpec((tt, LANES), lambda p, i: (i * p, 0))
    kern = functools.partial(_route_kernel, n_experts=n_experts, tm=tm, tt=tt)
    return pl.pallas_call(
        kern,
        grid=(2, t // tt),
        in_specs=[pl.BlockSpec((tt, LANES), lambda p, i: (i, 0))],
        out_specs=[tile(), tile(), pl.BlockSpec((8, LANES), lambda p, i: (0, 0))],
        out_shape=[jax.ShapeDtypeStruct((t, LANES), jnp.int32), jax.ShapeDtypeStruct((t, LANES), F32),
                   jax.ShapeDtypeStruct((8, LANES), jnp.int32)],
        scratch_shapes=[pltpu.VMEM((1, LANES), F32), pltpu.VMEM((1, LANES), F32), pltpu.VMEM((1, LANES), F32)],
        compiler_params=_cparams(("arbitrary", "arbitrary")),
        name="moe_route",
    )(logits)


def _row_copy(src, si, dst, di, sem):
    return pltpu.make_async_copy(src.at[pl.ds(si, 1), :], dst.at[pl.ds(di, 1), :], sem)


def _dispatch_kernel(cnt_ref, pst_ref, nu_ref, pos_ref, x_ref, xs_ref, zero_scr, sem, *, n_experts, tm, tt):
    i = pl.program_id(0)
    p_rows = xs_ref.shape[0]

    @pl.when(i == 0)
    def _():
        zero_scr[...] = jnp.zeros_like(zero_scr)
        zrows = zero_scr.shape[0]

        def pad_copy(e):
            start = jnp.minimum((pst_ref[e] + cnt_ref[e]) // SUBLANES * SUBLANES, p_rows - zrows)
            return pltpu.make_async_copy(zero_scr, xs_ref.at[pl.ds(pl.multiple_of(start, SUBLANES), zrows), :], sem)

        def start_pad(e, c):
            pad_copy(e).start()
            return c

        def wait_pad(e, c):
            pad_copy(e).wait()
            return c

        lax.fori_loop(0, n_experts, start_pad, 0)
        lax.fori_loop(0, n_experts, wait_pad, 0)

        def zero_tail(blk, c):
            cp = pltpu.make_async_copy(zero_scr.at[pl.ds(0, tm), :],
                                       xs_ref.at[pl.ds(pl.multiple_of(blk * tm, tm), tm), :], sem)
            cp.start()
            cp.wait()
            return c

        lax.fori_loop(nu_ref[0], p_rows // tm, zero_tail, 0)

    def start_rows(r8, c):
        for rr in range(SUBLANES):
            for k in range(TOP_K):
                p = pos_ref[(r8 * SUBLANES + rr) * TOP_K + k]
                _row_copy(x_ref.at[r8], rr, xs_ref, p, sem).start()
        return c

    def wait_rows(r, c):
        for k in range(TOP_K):
            _row_copy(x_ref.at[0], 0, xs_ref, 0, sem).wait()
        return c

    lax.fori_loop(0, tt // SUBLANES, start_rows, 0)
    lax.fori_loop(0, tt, wait_rows, 0, unroll=2)


def moe_dispatch(xpk, pos_flat, cnt, pst, n_used, p_rows, *, n_experts, tm, tt):
    t, dh = xpk.shape
    kern = functools.partial(_dispatch_kernel, n_experts=n_experts, tm=tm, tt=tt)
    return pl.pallas_call(
        kern,
        grid_spec=pltpu.PrefetchScalarGridSpec(
            num_scalar_prefetch=3,
            grid=(t // tt,),
            in_specs=[pl.BlockSpec((tt * TOP_K,), lambda i, *_: (i,), memory_space=pltpu.SMEM),
                      pl.BlockSpec((tt // SUBLANES, SUBLANES, dh), lambda i, *_: (i, 0, 0))],
            out_specs=pl.BlockSpec(memory_space=pl.ANY),
            scratch_shapes=[pltpu.VMEM((tm + SUBLANES, dh), U32), pltpu.SemaphoreType.DMA(())],
        ),
        out_shape=jax.ShapeDtypeStruct((p_rows, dh), U32),
        compiler_params=_cparams(("arbitrary",)),
        name="moe_dispatch",
    )(cnt, pst, n_used, pos_flat, xpk.reshape(t // SUBLANES, SUBLANES, dh))


def _new_expert(be_ref, i):
    prev = be_ref[jnp.maximum(i - 1, 0)]
    return (i == 0) | (be_ref[i] != prev)


def _by_block_rows(rows, tm, compute, out_ref):
    half = tm // 2

    def part():
        compute(half)
        out_ref[half:, :] = jnp.zeros((tm - half, out_ref.shape[1]), out_ref.dtype)

    pl.when(rows > half)(functools.partial(compute, tm))
    pl.when((rows > 0) & (rows <= half))(part)

    @pl.when(rows == 0)
    def _():
        out_ref[...] = jnp.zeros_like(out_ref)


def _moe_gu_kernel(be_ref, br_ref, x_ref, wg_ref, wu_ref, bg_ref, bu_ref, a_ref, wgb, wub):
    i = pl.program_id(1)
    tm, hw = x_ref.shape

    @pl.when(_new_expert(be_ref, i))
    def _():
        wgb[...] = wg_ref[...].astype(BF16)
        wub[...] = wu_ref[...].astype(BF16)

    def compute(n):
        lo, hi = _unpack_halves(x_ref[:n, :])
        gate = _dot(lo, wgb[:hw, :]) + _dot(hi, wgb[hw:, :]) + bg_ref[...]
        up = _dot(lo, wub[:hw, :]) + _dot(hi, wub[hw:, :]) + bu_ref[...]
        gate = jnp.minimum(gate, SWIGLU_LIMIT)
        up = jnp.clip(up, -SWIGLU_LIMIT, SWIGLU_LIMIT)
        sig = 1.0 / (1.0 + jnp.exp(-SWIGLU_ALPHA * gate))
        a_ref[:n, :] = (gate * sig * (up + 1.0)).astype(a_ref.dtype)

    _by_block_rows(br_ref[i], tm, compute, a_ref)


def _used_block(i, br):
    return jnp.where(br[i] > 0, i, 0)


def moe_gate_up(xs, w_gu, b_gu, layer, blk_e, blk_rows, *, tm, tn):
    p, hw = xs.shape
    d = 2 * hw
    f = w_gu.shape[3] // 2
    assert f % tn == 0 and p % tm == 0
    nj = f // tn
    wspec = lambda off: pl.BlockSpec((None, None, d, tn), lambda j, i, be, nu: (layer, be[i], 0, off + j))
    bspec = lambda off: pl.BlockSpec((None, None, 1, tn), lambda j, i, be, nu: (layer, be[i], 0, off + j))
    return pl.pallas_call(
        _moe_gu_kernel,
        grid_spec=pltpu.PrefetchScalarGridSpec(
            num_scalar_prefetch=2,
            grid=(nj, p // tm),
            in_specs=[pl.BlockSpec((tm, hw), lambda j, i, be, nu: (_used_block(i, nu), 0)),
                      wspec(0), wspec(nj), bspec(0), bspec(nj)],
            out_specs=pl.BlockSpec((tm, tn), lambda j, i, be, nu: (i, j)),
            scratch_shapes=[pltpu.VMEM((d, tn), BF16), pltpu.VMEM((d, tn), BF16)],
        ),
        out_shape=jax.ShapeDtypeStruct((p, f), BF16),
        compiler_params=_cparams(("arbitrary", "arbitrary")),
        name="moe_gate_up",
    )(blk_e, blk_rows, xs, w_gu, w_gu, b_gu, b_gu)


def _moe_down_kernel(be_ref, br_ref, a_ref, w_ref, b_ref, y_ref, wb):
    i = pl.program_id(1)

    @pl.when(_new_expert(be_ref, i))
    def _():
        wb[...] = w_ref[...].astype(BF16)

    def compute(n):
        y_ref[:n, :] = _dot(a_ref[:n, :], wb[...]) + b_ref[...]

    _by_block_rows(br_ref[i], a_ref.shape[0], compute, y_ref)


def moe_down(a, w_down, b_down, layer, blk_e, blk_rows, *, tm, tn):
    p, f = a.shape
    d = w_down.shape[3]
    assert d % tn == 0 and p % tm == 0
    return pl.pallas_call(
        _moe_down_kernel,
        grid_spec=pltpu.PrefetchScalarGridSpec(
            num_scalar_prefetch=2,
            grid=(d // tn, p // tm),
            in_specs=[pl.BlockSpec((tm, f), lambda j, i, be, nu: (i, 0)),
                      pl.BlockSpec((None, None, f, tn), lambda j, i, be, nu: (layer, be[i], 0, j)),
                      pl.BlockSpec((None, None, 1, tn), lambda j, i, be, nu: (layer, be[i], 0, j))],
            out_specs=pl.BlockSpec((tm, tn), lambda j, i, be, nu: (i, j)),
            scratch_shapes=[pltpu.VMEM((f, tn), BF16)],
        ),
        out_shape=jax.ShapeDtypeStruct((p, d), F32),
        compiler_params=_cparams(("arbitrary", "arbitrary")),
        name="moe_down",
    )(blk_e, blk_rows, a, w_down, b_down)


def _combine_kernel(pos_ref, posn_ref, gate_ref, x_ref, g_ref, b_ref, ys_ref, o_ref, ob_ref, buf, sem,
                    *, alpha, tt):
    i = pl.program_id(0)
    n = pl.num_programs(0)
    slot = i % 2

    def start_tile(p_ref, s):
        def body(r8, c):
            for rr in range(SUBLANES):
                for k in range(TOP_K):
                    p = p_ref[(r8 * SUBLANES + rr) * TOP_K + k]
                    _row_copy(ys_ref, p, buf.at[s, k, r8], rr, sem.at[s]).start()
            return c
        lax.fori_loop(0, tt // SUBLANES, body, 0)

    @pl.when(i == 0)
    def _():
        start_tile(pos_ref, 0)

    @pl.when(i + 1 < n)
    def _():
        start_tile(posn_ref, 1 - slot)

    def wait_rows(r, c):
        for k in range(TOP_K):
            _row_copy(ys_ref, 0, buf.at[slot, k, 0], 0, sem.at[slot]).wait()
        return c

    lax.fori_loop(0, tt, wait_rows, 0, unroll=2)
    gates = gate_ref[...]
    d = x_ref.shape[1]
    f = gates[:, 0:1] * buf[slot, 0].reshape(tt, d)
    for k in range(1, TOP_K):
        f = f + gates[:, k:k + 1] * buf[slot, k].reshape(tt, d)
    y = _layer_norm(alpha * x_ref[...] + f, g_ref[...], b_ref[...])
    o_ref[...] = y
    ob_ref[...] = y.astype(BF16)


def moe_combine_ln(ys, pos_flat, gates, x, g, b, *, alpha, tt):
    t, d = x.shape
    nt = t // tt
    tile = pl.BlockSpec((tt, d), lambda i: (i, 0))
    row = pl.BlockSpec((1, d), lambda i: (0, 0))
    kern = functools.partial(_combine_kernel, alpha=alpha, tt=tt)
    return pl.pallas_call(
        kern,
        grid=(nt,),
        in_specs=[pl.BlockSpec((tt * TOP_K,), lambda i: (i,), memory_space=pltpu.SMEM),
                  pl.BlockSpec((tt * TOP_K,), lambda i: (jnp.minimum(i + 1, nt - 1),), memory_space=pltpu.SMEM),
                  pl.BlockSpec((tt, LANES), lambda i: (i, 0)),
                  tile, row, row,
                  pl.BlockSpec(memory_space=pl.ANY)],
        out_specs=[tile, tile],
        out_shape=[jax.ShapeDtypeStruct((t, d), F32), jax.ShapeDtypeStruct((t, d), BF16)],
        scratch_shapes=[pltpu.VMEM((2, TOP_K, tt // SUBLANES, SUBLANES, d), F32),
                        pltpu.SemaphoreType.DMA((2,))],
        compiler_params=_cparams(("arbitrary",)),
        name="moe_combine_ln",
    )(pos_flat, pos_flat, gates, x, g, b, ys)


def kernel(x_prompt, x_sample, mem_prompt, mem_sample, ev_w_in, ev_w_out, ev_sink, ev_lam_q1, ev_lam_k1, ev_lam_q2, ev_lam_k2, ev_dnorm_w, od_w_in, od_w_out, od_conv_w, mem_wq, mem_wkv, mem_wo, ln_g, ln_b, moe_router_w, moe_router_b, moe_w_gu, moe_b_gu, moe_w_down, moe_b_down):
    b1, s1, d = x_prompt.shape
    b2, s2, _ = x_sample.shape
    segs = Segs(b1, s1, b2, s2)
    t = segs.t
    depth = ln_g.shape[0]
    n_mem = mem_prompt.shape[1]
    n_experts = moe_router_w.shape[2]
    alpha = (2 * depth) ** 0.25
    n_heads = d // (2 * HEAD_DIM)
    n_kv = n_heads // 4
    c_ch = d // 2
    dils = tuple(dil for _, dil in D_PATTERNS)

    x = jnp.concatenate([x_prompt.reshape(-1, d), x_sample.reshape(-1, d)], axis=0)
    xb = x.astype(BF16)
    memb = jnp.concatenate([mem_prompt.reshape(-1, d), mem_sample.reshape(-1, d)], axis=0).astype(BF16)
    tables = rope_tables(max(s1, s2))
    dtabs = dilated_tables(tables)
    ev_modes = jnp.array([MODE_ROPE128_Q] * n_heads + [MODE_ROPE128] * n_kv + [MODE_NONE] * n_kv
                         + [MODE_ROPE64_Q] * n_heads + [MODE_ROPE64] * n_heads
                         + [MODE_NONE] * n_heads, jnp.int32)
    od_modes = jnp.zeros((3 * c_ch // LANES,), jnp.int32)
    mem_modes = jnp.zeros((mem_wkv.shape[2] // LANES,), jnp.int32)
    mem_segs = Segs(b1, n_mem, b2, n_mem)
    rw_pad = jnp.pad(moe_router_w, ((0, 0), (0, 0), (0, LANES - n_experts)))
    rb_pad = jnp.pad(moe_router_b, ((0, 0), (0, LANES - n_experts)))[:, None, :]
    b_gu4 = moe_b_gu[:, :, None, :]
    b_down4 = moe_b_down[:, :, None, :]
    moe_tm = 512
    n_blocks = -(-(t * TOP_K + n_experts * (moe_tm - 1)) // moe_tm)
    p_rows = n_blocks * moe_tm

    for l in range(depth):
        j = l // 2
        if l % 2 == 0:
            lambda_init = 0.8 - 0.6 * math.exp(-0.3 * l)
            w_f = ev_w_in[j]
            kdim = w_f.shape[0]
            nb_lo, nb_hi = (n_heads + 2 * n_kv) * LANES, (3 * n_heads + 2 * n_kv) * LANES
            w_bqk = w_f[:, nb_lo:nb_hi].reshape(kdim, 2 * n_heads, 2, 2, B_DIM // 2)
            w_bqk = w_bqk.swapaxes(2, 3).reshape(kdim, nb_hi - nb_lo)
            w_in = jnp.concatenate([w_f[:, :nb_lo], w_bqk, w_f[:, nb_hi:]], axis=1).astype(BF16)
            n_in = w_in.shape[1]
            tn = n_in // 6 if (n_in // 6) % LANES == 0 else n_in // 18
            h = proj(xb, w_in, ev_modes, tables, segs, col0=0, ncols=n_in, tm=1024, tn=tn,
                     out_dtype=BF16)
            lamv = jnp.stack([ev_lam_q1[j], ev_lam_k1[j], ev_lam_q2[j], ev_lam_k2[j]])
            oa, ob = [], []
            for row0, bsz, s in segs.groups():
                oa.append(window_attention(h, ev_sink[j], row0, bsz, s, d_model=d))
                ob.append(diff_attention(h, lamv, ev_dnorm_w[j][None, :], row0, bsz, s, lambda_init,
                                         d_model=d, tq=1024, tk=2048, n_streams=4))
            fs = [jnp.concatenate(oa, axis=0), jnp.concatenate(ob, axis=0)]
            w_out = ev_w_out[j].astype(BF16)
        else:
            w_in = od_w_in[j].astype(BF16)
            hc = proj(xb, w_in, od_modes, tables, segs, col0=0, ncols=3 * c_ch, tm=1024,
                      tn=c_ch, out_dtype=F32)
            hqs = [proj_dilated(xb, w_in, dtabs, segs, g, dil, col0=3 * c_ch, tm=1024)
                   for g, dil in enumerate(dils)]
            oc, od = [], []
            for row0, bsz, s in segs.groups():
                oc.append(gated_conv(hc, od_conv_w[j], row0, bsz, s, ts=512, tc=c_ch // 2))
                outs = [dilated_group(hq, dil, row0, bsz, s) for hq, dil in zip(hqs, dils)]
                od.append(dilated_combine([o for o, _ in outs], [ls for _, ls in outs], dils, tm=512))
            fs = [jnp.concatenate(oc, axis=0), jnp.concatenate(od, axis=0)]
            w_out = od_w_out[j].astype(BF16)
        x, xb = outproj_ln(fs, w_out, x, ln_g[l, 0][None, :], ln_b[l, 0][None, :], alpha=alpha, tm=512)

        kv = proj(memb, mem_wkv[l].astype(BF16), mem_modes, tables, mem_segs, col0=0,
                  ncols=mem_wkv.shape[2], tm=n_mem, tn=mem_wkv.shape[2] // 2, out_dtype=BF16)
        x, xpk, logits = mem_attention_ln(xb, x, kv, mem_wq[l].astype(BF16), mem_wo[l].astype(BF16),
                                          ln_g[l, 1][None, :], ln_b[l, 1][None, :], rw_pad[l], rb_pad[l],
                                          segs, alpha=alpha, tm=512, n_mem=n_mem)

        pos, gates, meta = route(logits, n_experts, tm=moe_tm, tt=512)
        pos_flat = pos[:, :TOP_K].reshape(-1)
        cnt, pst, end_blk = meta[0, :n_experts], meta[1, :n_experts], meta[2, :n_experts]
        blk_e = jnp.minimum(jnp.sum(end_blk[None, :] <= jnp.arange(n_blocks, dtype=jnp.int32)[:, None], axis=1),
                            n_experts - 1).astype(jnp.int32)
        n_used = end_blk[n_experts - 1:]
        blk_lo = (jnp.arange(n_blocks, dtype=jnp.int32) * moe_tm)[:, None]
        blk_rows = jnp.sum(jnp.clip(jnp.minimum((pst + cnt)[None, :], blk_lo + moe_tm)
                                    - jnp.maximum(pst[None, :], blk_lo), 0, moe_tm), axis=1).astype(jnp.int32)
        xs = moe_dispatch(xpk, pos_flat, cnt, pst, n_used, p_rows, n_experts=n_experts, tm=moe_tm, tt=256)
        a = moe_gate_up(xs, moe_w_gu, b_gu4, l, blk_e, blk_rows, tm=moe_tm, tn=1024)
        ys = moe_down(a, moe_w_down, b_down4, l, blk_e, blk_rows, tm=moe_tm, tn=d)
        x, xb = moe_combine_ln(ys, pos_flat, gates, x, ln_g[l, 2][None, :], ln_b[l, 2][None, :],
                               alpha=alpha, tt=256)

    y1 = x[:segs.t1].reshape(b1, s1, d)
    y2 = x[segs.t1:].reshape(b2, s2, d)
    return (y1, y2)
```

```python
import functools
import math

import jax
import jax.numpy as jnp
from jax import lax
from jax.experimental import pallas as pl
from jax.experimental.pallas import tpu as pltpu

F32 = jnp.float32
BF16 = jnp.bfloat16
U32 = jnp.uint32
LANES = 128
SUBLANES = 8
VMEM_LIMIT = 56 * 1024 * 1024

HEAD_DIM = 128
ROPE_THETA = 10000.0
A_WINDOW = 128
B_DIM = 64
C_WIDTH = 3
D_PATTERNS = ((128, 1), (512, 4), (2048, 16))
D_HEADS_PER_GROUP = 4
D_NK = 64
MEM_HEADS = 4
TOP_K = 4
SWIGLU_LIMIT = 7.0
SWIGLU_ALPHA = 1.702
LN_EPS = 1e-5
NEG_INF = -1e30

MODE_NONE, MODE_ROPE128, MODE_ROPE128_Q, MODE_ROPE64, MODE_ROPE64_Q = 0, 1, 2, 3, 4


def _cparams(sem):
    return pltpu.CompilerParams(dimension_semantics=sem, vmem_limit_bytes=VMEM_LIMIT)


def _dot(a, b):
    return jnp.dot(a, b, preferred_element_type=F32)


def _dot_t(a, b):
    return lax.dot_general(a, b, (((1,), (1,)), ((), ())), preferred_element_type=F32)


class Segs:
    def __init__(self, b1, s1, b2, s2):
        self.b1, self.s1, self.b2, self.s2 = b1, s1, b2, s2
        self.t1 = b1 * s1
        self.t = b1 * s1 + b2 * s2

    def groups(self):
        return ((0, self.b1, self.s1), (self.t1, self.b2, self.s2))

    def pos_tile(self, i, tm):
        n1 = self.t1 // tm
        return jnp.where(i < n1, i % (self.s1 // tm), (i - n1) % (self.s2 // tm))

    def batch_of_tile(self, i, tm):
        n1 = self.t1 // tm
        return jnp.where(i < n1, (i * tm) // self.s1, self.b1 + ((i - n1) * tm) // self.s2)


def _layer_norm(z, g, b):
    mu = jnp.mean(z, axis=-1, keepdims=True)
    zc = z - mu
    var = jnp.mean(zc * zc, axis=-1, keepdims=True)
    return zc * lax.rsqrt(var + LN_EPS) * g + b


def _rope128(a, c, s):
    return a * c + pltpu.roll(a, 64, 1) * s


def _proj_kernel(mode_ref, x_ref, w_ref, c128_ref, s128_ref, c64_ref, s64_ref, o_ref,
                 *, n_slabs, slab0, q128_scale, q64_scale):
    j = pl.program_id(0)
    acc = _dot(x_ref[...], w_ref[...])
    for s in range(n_slabs):
        a = acc[:, s * LANES:(s + 1) * LANES]
        mode = mode_ref[slab0 + j * n_slabs + s]
        w0 = jnp.where(mode == MODE_NONE, 1.0, 0.0)
        w128 = jnp.where(mode == MODE_ROPE128, 1.0, jnp.where(mode == MODE_ROPE128_Q, q128_scale, 0.0))
        w64 = jnp.where(mode == MODE_ROPE64, 1.0, jnp.where(mode == MODE_ROPE64_Q, q64_scale, 0.0))
        c = w0 + w128 * c128_ref[...] + w64 * c64_ref[...]
        sn = w128 * s128_ref[...] + w64 * s64_ref[...]
        o_ref[:, s * LANES:(s + 1) * LANES] = _rope128(a, c, sn).astype(o_ref.dtype)


def proj(xb, w, modes, tables, segs, *, col0, ncols, tm, tn, out_dtype):
    m, k = xb.shape
    assert ncols % tn == 0 and col0 % tn == 0 and m % tm == 0 and tn % LANES == 0
    n_slabs = tn // LANES
    jb = col0 // tn
    tab_spec = pl.BlockSpec((tm, LANES), lambda j, i, *_: (segs.pos_tile(i, tm), 0))
    kern = functools.partial(_proj_kernel, n_slabs=n_slabs, slab0=col0 // LANES,
                             q128_scale=HEAD_DIM ** -0.5, q64_scale=B_DIM ** -0.5)
    return pl.pallas_call(
        kern,
        grid_spec=pltpu.PrefetchScalarGridSpec(
            num_scalar_prefetch=1,
            grid=(ncols // tn, m // tm),
            in_specs=[pl.BlockSpec((tm, k), lambda j, i, *_: (i, 0)),
                      pl.BlockSpec((k, tn), lambda j, i, *_: (0, jb + j)),
                      tab_spec, tab_spec, tab_spec, tab_spec],
            out_specs=pl.BlockSpec((tm, tn), lambda j, i, *_: (i, j)),
        ),
        out_shape=jax.ShapeDtypeStruct((m, ncols), out_dtype),
        compiler_params=_cparams(("arbitrary", "arbitrary")),
        name="proj",
    )(modes, xb, w, *tables)


def rope_tables(s):
    inv = ROPE_THETA ** (-jnp.arange(0, HEAD_DIM, 2, dtype=F32) / HEAD_DIM)
    ang = jnp.arange(s, dtype=F32)[:, None] * inv[None, :]
    c, sn = jnp.cos(ang), jnp.sin(ang)
    c128 = jnp.concatenate([c, c], axis=1)
    s128 = jnp.concatenate([-sn, sn], axis=1)
    invb = ROPE_THETA ** (-jnp.arange(0, B_DIM, 2, dtype=F32) / B_DIM)
    angb = jnp.arange(s, dtype=F32)[:, None] * invb[None, :]
    cb, sb = jnp.cos(angb), jnp.sin(angb)
    c64 = jnp.concatenate([cb, cb, cb, cb], axis=1)
    s64 = jnp.concatenate([-sb, -sb, sb, sb], axis=1)
    return (c128, s128, c64, s64)


B_SLAB_PERM = tuple(list(range(0, 32)) + list(range(64, 96)) + list(range(32, 64)) + list(range(96, 128)))


def _is_comp0(lane):
    return (lane // (B_DIM // 2)) % 2 == 0


def _window_kernel(sink_ref, q_ref, kp_ref, kc_ref, kn_ref, vp_ref, vc_ref, vn_ref, o_ref,
                   *, n_heads, n_kv, blk):
    n = pl.program_id(1)
    nb = pl.num_programs(1)
    r = lax.broadcasted_iota(jnp.int32, (blk, 3 * blk), 0)
    c = lax.broadcasted_iota(jnp.int32, (blk, 3 * blk), 1)
    valid = jnp.abs(r + blk - c) <= A_WINDOW
    valid &= (c >= blk) | (n > 0)
    valid &= (c < 2 * blk) | (n < nb - 1)
    g = n_heads // n_kv
    for kv in range(n_kv):
        cs = slice(kv * HEAD_DIM, (kv + 1) * HEAD_DIM)
        k3 = jnp.concatenate([kp_ref[:, cs], kc_ref[:, cs], kn_ref[:, cs]], axis=0)
        v3 = jnp.concatenate([vp_ref[:, cs], vc_ref[:, cs], vn_ref[:, cs]], axis=0)
        for gi in range(g):
            h = kv * g + gi
            hs = slice(h * HEAD_DIM, (h + 1) * HEAD_DIM)
            sc = jnp.where(valid, _dot_t(q_ref[:, hs], k3), NEG_INF)
            sk = sink_ref[h]
            m = jnp.maximum(jnp.max(sc, axis=-1, keepdims=True), sk)
            p = jnp.exp(sc - m)
            denom = jnp.sum(p, axis=-1, keepdims=True) + jnp.exp(sk - m)
            o = _dot(p.astype(BF16), v3) / denom
            o_ref[:, hs] = o.astype(o_ref.dtype)


def window_attention(h, sink, row0, bsz, s, *, d_model):
    blk = A_WINDOW
    n_heads = d_model // (2 * HEAD_DIM)
    n_kv = n_heads // 4
    qw, kw = n_heads * HEAD_DIM, n_kv * HEAD_DIM
    nb = s // blk
    rb0 = row0 // blk
    kcol, vcol = qw // kw, qw // kw + 1

    def rows(b, n):
        return rb0 + b * nb + n

    def spec(col, shift):
        def imap(b, n):
            nn = jnp.clip(n + shift, 0, nb - 1)
            return (rows(b, nn), col)
        return pl.BlockSpec((blk, kw), imap)

    kern = functools.partial(_window_kernel, n_heads=n_heads, n_kv=n_kv, blk=blk)
    return pl.pallas_call(
        kern,
        grid=(bsz, nb),
        in_specs=[pl.BlockSpec(memory_space=pltpu.SMEM),
                  pl.BlockSpec((blk, qw), lambda b, n: (rows(b, n), 0)),
                  spec(kcol, -1), spec(kcol, 0), spec(kcol, 1),
                  spec(vcol, -1), spec(vcol, 0), spec(vcol, 1)],
        out_specs=pl.BlockSpec((blk, qw), lambda b, n: (b * nb + n, 0)),
        out_shape=jax.ShapeDtypeStruct((bsz * s, qw), BF16),
        compiler_params=_cparams(("arbitrary", "arbitrary")),
        name="window_attn",
    )(sink, h, h, h, h, h, h, h)


def _diff_kernel(lamv_ref, nw_ref, q_ref, k_ref, v_ref, o_ref, *, tq, tk, n_streams, lambda_init):
    s = k_ref.shape[0]
    ts = tq // n_streams
    lane = lax.broadcasted_iota(jnp.int32, (ts, LANES), 1)
    q2s = []
    for st in range(n_streams):
        q = q_ref[st * ts:(st + 1) * ts, :]
        zero = jnp.zeros_like(q)
        q2s.append(jnp.concatenate([jnp.where(_is_comp0(lane), q, zero),
                                    jnp.where(_is_comp0(lane), zero, q)], axis=0))

    def body(i, carry):
        ks = pl.multiple_of(i * tk, tk)
        kb = k_ref[pl.ds(ks, tk), :]
        vb = v_ref[pl.ds(ks, tk), :]
        out = []
        for st in range(n_streams):
            m, l, acc = carry[st]
            sc = _dot_t(q2s[st], kb)
            m_new = jnp.maximum(m, jnp.max(sc, axis=-1, keepdims=True))
            a = jnp.exp(m - m_new)
            p = jnp.exp(sc - m_new)
            l = a * l + jnp.sum(p, axis=-1, keepdims=True)
            acc = a * acc + _dot(p.astype(BF16), vb)
            out.append((m_new, l, acc))
        return tuple(out)

    init = tuple((jnp.full((2 * ts, 1), NEG_INF, F32), jnp.zeros((2 * ts, 1), F32),
                  jnp.zeros((2 * ts, LANES), F32)) for _ in range(n_streams))
    res = lax.fori_loop(0, s // tk, body, init)
    lv = lamv_ref[...]
    lam = (jnp.exp(jnp.sum(lv[0:1] * lv[1:2], axis=-1, keepdims=True))
           - jnp.exp(jnp.sum(lv[2:3] * lv[3:4], axis=-1, keepdims=True)) + lambda_init)
    for st in range(n_streams):
        m, l, acc = res[st]
        on = acc / l
        of = on[:ts] - lam * on[ts:]
        of = of * lax.rsqrt(jnp.mean(of * of, axis=-1, keepdims=True) + LN_EPS) * nw_ref[...]
        o_ref[st * ts:(st + 1) * ts, :] = (of * (1.0 - lambda_init)).astype(o_ref.dtype)


def diff_attention(h, lamv, norm_w, row0, bsz, s, lambda_init, *, d_model, tq, tk, n_streams):
    n_heads = d_model // (2 * HEAD_DIM)
    a_cols = (n_heads + 2 * (n_heads // 4))
    qc, kc, vc = a_cols, a_cols + n_heads, a_cols + 2 * n_heads
    nq = s // tq
    assert row0 % s == 0
    sb0 = row0 // s
    qb0 = row0 // tq
    kern = functools.partial(_diff_kernel, tq=tq, tk=tk, n_streams=n_streams, lambda_init=lambda_init)
    return pl.pallas_call(
        kern,
        grid=(bsz, n_heads, nq),
        in_specs=[pl.BlockSpec((4, B_DIM), lambda b, hh, n: (0, 0)),
                  pl.BlockSpec((1, LANES), lambda b, hh, n: (0, 0)),
                  pl.BlockSpec((tq, LANES), lambda b, hh, n: (qb0 + b * nq + n, qc + hh)),
                  pl.BlockSpec((s, LANES), lambda b, hh, n: (sb0 + b, kc + hh)),
                  pl.BlockSpec((s, LANES), lambda b, hh, n: (sb0 + b, vc + hh))],
        out_specs=pl.BlockSpec((tq, LANES), lambda b, hh, n: (b * nq + n, hh)),
        out_shape=jax.ShapeDtypeStruct((bsz * s, n_heads * LANES), BF16),
        compiler_params=_cparams(("arbitrary", "arbitrary", "arbitrary")),
        name="diff_attn",
    )(lamv, norm_w, h, h, h)


def _conv_kernel(cw_ref, b_ref, c_ref, x_ref, cp_ref, xp_ref, cn_ref, xn_ref, o_ref, *, ts):
    n = pl.program_id(1)
    nt = pl.num_programs(1)
    u = c_ref[...] * x_ref[...]
    prev = jnp.where(n > 0, cp_ref[7:8, :] * xp_ref[7:8, :], 0.0)
    nxt = jnp.where(n < nt - 1, cn_ref[0:1, :] * xn_ref[0:1, :], 0.0)
    row = lax.broadcasted_iota(jnp.int32, u.shape, 0)
    um1 = jnp.where(row == 0, prev, pltpu.roll(u, 1, 0))
    up1 = jnp.where(row == ts - 1, nxt, pltpu.roll(u, ts - 1, 0))
    conv = um1 * cw_ref[0:1, :] + u * cw_ref[1:2, :] + up1 * cw_ref[2:3, :]
    o_ref[...] = (b_ref[...] * conv).astype(o_ref.dtype)


def gated_conv(hc, conv_w, row0, bsz, s, *, ts, tc):
    c = hc.shape[1] // 3
    nt = s // ts
    nc = c // tc
    rb0 = row0 // ts
    r8 = ts // 8

    def main(col):
        return pl.BlockSpec((ts, tc), lambda b, n, j: (rb0 + b * nt + n, col * nc + j))

    def halo(col, shift):
        def imap(b, n, j):
            blk8 = (rb0 + b * nt + n) * r8 + (-1 if shift < 0 else r8)
            return (jnp.clip(blk8, 0, hc.shape[0] // 8 - 1), col * nc + j)
        return pl.BlockSpec((8, tc), imap)

    kern = functools.partial(_conv_kernel, ts=ts)
    return pl.pallas_call(
        kern,
        grid=(bsz, nt, nc),
        in_specs=[pl.BlockSpec((C_WIDTH, tc), lambda b, n, j: (0, j)),
                  main(0), main(1), main(2), halo(1, -1), halo(2, -1), halo(1, 1), halo(2, 1)],
        out_specs=pl.BlockSpec((ts, tc), lambda b, n, j: (b * nt + n, j)),
        out_shape=jax.ShapeDtypeStruct((bsz * s, c), BF16),
        compiler_params=_cparams(("arbitrary", "arbitrary", "arbitrary")),
        name="gated_conv",
    )(conv_w, hc, hc, hc, hc, hc, hc, hc)


def _proj_dil_kernel(x_ref, w_ref, c_ref, s_ref, o_ref, *scrs, dil, tm):
    rows = tm // dil
    acc = _dot(x_ref[...], w_ref[...])
    for sl, scr in enumerate(scrs):
        cols = slice(sl * LANES, (sl + 1) * LANES)
        a = _rope128(acc[:, cols], c_ref[...], s_ref[...])
        if dil == 1:
            o_ref[0, :, cols] = a.astype(o_ref.dtype)
        else:
            scr[...] = a
    if dil > 1:
        for r in range(dil):
            for sl, scr in enumerate(scrs):
                o_ref[r, :, sl * LANES:(sl + 1) * LANES] = scr[pl.ds(r, rows, stride=dil), :].astype(o_ref.dtype)


def dilated_tables(tables):
    c128, s128 = tables[0], tables[1]
    q_scale = HEAD_DIM ** -0.5
    return (jnp.stack([c128 * q_scale, c128, jnp.ones_like(c128)]),
            jnp.stack([s128 * q_scale, s128, jnp.zeros_like(s128)]))


def proj_dilated(xb, w, dtabs, segs, g, dil, *, col0, tm):
    m, k = xb.shape
    gw = D_HEADS_PER_GROUP * HEAD_DIM
    n_groups = len(D_PATTERNS)
    cb0 = col0 // gw
    tab = pl.BlockSpec((None, tm, LANES), lambda j, i: (j, segs.pos_tile(i, tm), 0))
    kern = functools.partial(_proj_dil_kernel, dil=dil, tm=tm)
    return pl.pallas_call(
        kern,
        grid=(3, m // tm),
        in_specs=[pl.BlockSpec((tm, k), lambda j, i: (i, 0)),
                  pl.BlockSpec((k, gw), lambda j, i: (0, cb0 + j * n_groups + g)),
                  tab, tab],
        out_specs=pl.BlockSpec((dil, tm // dil, gw), lambda j, i: (0, i, j)),
        out_shape=jax.ShapeDtypeStruct((dil, m // dil, 3 * gw), BF16),
        scratch_shapes=[pltpu.VMEM((tm, LANES), F32) for _ in range(gw // LANES)],
        compiler_params=_cparams(("arbitrary", "arbitrary")),
        name="proj_dilated",
    )(xb, w, dtabs[0], dtabs[1])


def _dilated_kernel(q_ref, kp_ref, kc_ref, kn_ref, vp_ref, vc_ref, vn_ref, o_ref, lse_ref,
                    *, blk, half):
    n = pl.program_id(2)
    nt = pl.num_programs(2)
    wk = blk + 2 * half
    r = lax.broadcasted_iota(jnp.int32, (blk, wk), 0)
    c = lax.broadcasted_iota(jnp.int32, (blk, wk), 1)
    valid = (c >= r) & (c <= r + 2 * half)
    valid &= (c >= half) | (n > 0)
    valid &= (c < half + blk) | (n < nt - 1)
    lane = lax.broadcasted_iota(jnp.int32, (blk, LANES), 1)
    lse_all = jnp.zeros((blk, LANES), F32)
    per = LANES // D_HEADS_PER_GROUP
    for hh in range(D_HEADS_PER_GROUP):
        hs = slice(hh * HEAD_DIM, (hh + 1) * HEAD_DIM)
        kk = jnp.concatenate([kp_ref[:, hs], kc_ref[:, hs], kn_ref[:, hs]], axis=0)
        vv = jnp.concatenate([vp_ref[:, hs], vc_ref[:, hs], vn_ref[:, hs]], axis=0)
        sc = jnp.where(valid, _dot_t(q_ref[:, hs], kk), NEG_INF)
        m = jnp.max(sc, axis=-1, keepdims=True)
        p = jnp.exp(sc - m)
        l = jnp.sum(p, axis=-1, keepdims=True)
        o_ref[:, hs] = _dot(p.astype(BF16), vv) / l
        lse_all = jnp.where(lane // per == hh, m + jnp.log(l), lse_all)
    lse_ref[...] = lse_all


def dilated_group(hq, dil, row0, bsz, s):
    blk, half = 2 * D_NK, D_NK
    gw = D_HEADS_PER_GROUP * HEAD_DIM
    sd = s // dil
    nt = sd // blk
    assert (row0 // dil) % blk == 0 and sd % blk == 0
    rb0 = row0 // dil // blk
    hb = blk // half

    def cur(sec):
        return pl.BlockSpec((None, blk, gw), lambda b, r, n: (r, rb0 + b * nt + n, sec))

    def halo(sec, shift):
        def imap(b, r, n):
            nn = jnp.clip(n * hb + (-1 if shift < 0 else hb), 0, nt * hb - 1)
            return (r, (rb0 + b * nt) * hb + nn, sec)
        return pl.BlockSpec((None, half, gw), imap)

    kern = functools.partial(_dilated_kernel, blk=blk, half=half)
    return pl.pallas_call(
        kern,
        grid=(bsz, dil, nt),
        in_specs=[cur(0), halo(1, -1), cur(1), halo(1, 1), halo(2, -1), cur(2), halo(2, 1)],
        out_specs=[pl.BlockSpec((None, blk, gw), lambda b, r, n: (r, b * nt + n, 0)),
                   pl.BlockSpec((None, blk, LANES), lambda b, r, n: (r, b * nt + n, 0))],
        out_shape=[jax.ShapeDtypeStruct((dil, bsz * sd, gw), F32),
                   jax.ShapeDtypeStruct((dil, bsz * sd, LANES), F32)],
        compiler_params=_cparams(("arbitrary", "arbitrary", "arbitrary")),
        name="dilated_attn",
    )(hq, hq, hq, hq, hq, hq, hq)


def _dil_combine_kernel(*refs, dils, tm):
    ng = len(dils)
    nh = D_HEADS_PER_GROUP
    o_refs, l_refs = refs[:ng], refs[ng:2 * ng]
    out_ref = refs[2 * ng]
    scr = refs[2 * ng + 1:]
    o_scr = [scr[g * (nh + 1):g * (nh + 1) + nh] for g in range(ng)]
    l_scr = [scr[g * (nh + 1) + nh] for g in range(ng)]
    for o_ref, l_ref, osc, lsc, dil in zip(o_refs, l_refs, o_scr, l_scr, dils):
        for r in range(dil):
            rs = pl.ds(r, tm // dil, stride=dil) if dil > 1 else pl.ds(0, tm)
            for hh in range(nh):
                osc[hh][rs, :] = o_ref[r, :, hh * HEAD_DIM:(hh + 1) * HEAD_DIM]
            lsc[rs, :] = l_ref[r]
    per = LANES // nh
    for hh in range(nh):
        hs = slice(hh * HEAD_DIM, (hh + 1) * HEAD_DIM)
        ls = slice(hh * per, hh * per + 1)
        lses = [lsc[:, ls] for lsc in l_scr]
        m = functools.reduce(jnp.maximum, lses)
        ws = [jnp.exp(a - m) for a in lses]
        den = functools.reduce(lambda a, b: a + b, ws)
        o = functools.reduce(lambda a, b: a + b, [(w / den) * osc[hh][...] for w, osc in zip(ws, o_scr)])
        out_ref[:, hs] = o.astype(out_ref.dtype)


def dilated_combine(os_, lses, dils, *, tm):
    gw = os_[0].shape[2]
    m = os_[0].shape[0] * os_[0].shape[1]
    ospecs = [pl.BlockSpec((d, tm // d, gw), lambda i: (0, i, 0)) for d in dils]
    lspecs = [pl.BlockSpec((d, tm // d, LANES), lambda i: (0, i, 0)) for d in dils]
    kern = functools.partial(_dil_combine_kernel, dils=dils, tm=tm)
    return pl.pallas_call(
        kern,
        grid=(m // tm,),
        in_specs=ospecs + lspecs,
        out_specs=pl.BlockSpec((tm, gw), lambda i: (i, 0)),
        out_shape=jax.ShapeDtypeStruct((m, gw), BF16),
        scratch_shapes=[pltpu.VMEM((tm, LANES), F32) for _ in range(len(dils) * (D_HEADS_PER_GROUP + 1))],
        compiler_params=_cparams(("arbitrary",)),
        name="dilated_combine",
    )(*os_, *lses)


def _outproj_kernel(*refs, n_in, alpha):
    f_refs = refs[:n_in]
    w_refs = refs[n_in:2 * n_in]
    x_ref, g_ref, b_ref, o_ref, ob_ref = refs[2 * n_in:]
    acc = alpha * x_ref[...]
    for f_ref, w_ref in zip(f_refs, w_refs):
        acc = acc + _dot(f_ref[...], w_ref[...])
    y = _layer_norm(acc, g_ref[...], b_ref[...])
    o_ref[...] = y
    ob_ref[...] = y.astype(BF16)


def outproj_ln(fs, w, x, g, b, *, alpha, tm):
    m, d = x.shape
    n_in = len(fs)
    widths = [f.shape[1] for f in fs]
    in_specs = [pl.BlockSpec((tm, wd), lambda i: (i, 0)) for wd in widths]
    ws = []
    off = 0
    for wd in widths:
        assert off % wd == 0
        in_specs.append(pl.BlockSpec((wd, d), functools.partial(lambda i, o: (o, 0), o=off // wd)))
        ws.append(w)
        off += wd
    row = pl.BlockSpec((1, d), lambda i: (0, 0))
    in_specs += [pl.BlockSpec((tm, d), lambda i: (i, 0)), row, row]
    kern = functools.partial(_outproj_kernel, n_in=n_in, alpha=alpha)
    return pl.pallas_call(
        kern,
        grid=(m // tm,),
        in_specs=in_specs,
        out_specs=[pl.BlockSpec((tm, d), lambda i: (i, 0)), pl.BlockSpec((tm, d), lambda i: (i, 0))],
        out_shape=[jax.ShapeDtypeStruct((m, d), F32), jax.ShapeDtypeStruct((m, d), BF16)],
        compiler_params=_cparams(("arbitrary",)),
        name="outproj_ln",
    )(*fs, *ws, x, g, b)


def _pack_halves(y):
    bits = lax.bitcast_convert_type(y.astype(BF16).astype(F32), U32)
    hw = y.shape[1] // 2
    return (bits[:, hw:] & jnp.uint32(0xFFFF0000)) | (bits[:, :hw] >> 16)


def _unpack_halves(pk):
    lo = lax.bitcast_convert_type(pk << 16, F32).astype(BF16)
    hi = lax.bitcast_convert_type(pk & jnp.uint32(0xFFFF0000), F32).astype(BF16)
    return lo, hi


def _mem_kernel(xb_ref, x_ref, kv_ref, wq_ref, wo_ref, g_ref, b_ref, rw_ref, rb_ref,
                o_ref, pk_ref, lg_ref, *, alpha):
    inner = wq_ref.shape[1]
    hd = inner // MEM_HEADS
    q = (_dot(xb_ref[...], wq_ref[...]) * hd ** -0.5).astype(BF16)
    outs = []
    for hh in range(MEM_HEADS):
        hs = slice(hh * hd, (hh + 1) * hd)
        sc = _dot_t(q[:, hs], kv_ref[:, hs])
        m = jnp.max(sc, axis=-1, keepdims=True)
        p = jnp.exp(sc - m)
        l = jnp.sum(p, axis=-1, keepdims=True)
        outs.append(_dot(p.astype(BF16), kv_ref[:, inner + hh * hd:inner + (hh + 1) * hd]) / l)
    o = jnp.concatenate(outs, axis=1).astype(BF16)
    y = _layer_norm(alpha * x_ref[...] + _dot(o, wo_ref[...]), g_ref[...], b_ref[...])
    o_ref[...] = y
    pk_ref[...] = _pack_halves(y)
    yb = y.astype(BF16)
    yl = (y - yb.astype(F32)).astype(BF16)
    rw = rw_ref[...]
    rwh = rw.astype(BF16)
    rwl = (rw - rwh.astype(F32)).astype(BF16)
    lg_ref[...] = _dot(yb, rwh) + _dot(yl, rwh) + _dot(yb, rwl) + rb_ref[...]


def mem_attention_ln(xb, x, kv, wq, wo, g, b, rw, rb, segs, *, alpha, tm, n_mem):
    m, d = x.shape
    inner = wq.shape[1]
    full = lambda a: pl.BlockSpec(a.shape, lambda i: (0, 0))
    tile = pl.BlockSpec((tm, d), lambda i: (i, 0))
    kern = functools.partial(_mem_kernel, alpha=alpha)
    return pl.pallas_call(
        kern,
        grid=(m // tm,),
        in_specs=[tile, tile,
                  pl.BlockSpec((n_mem, 2 * inner), lambda i: (segs.batch_of_tile(i, tm), 0)),
                  full(wq), full(wo), full(g), full(b), full(rw), full(rb)],
        out_specs=[tile, pl.BlockSpec((tm, d // 2), lambda i: (i, 0)),
                   pl.BlockSpec((tm, LANES), lambda i: (i, 0))],
        out_shape=[jax.ShapeDtypeStruct((m, d), F32), jax.ShapeDtypeStruct((m, d // 2), U32),
                   jax.ShapeDtypeStruct((m, LANES), F32)],
        compiler_params=_cparams(("arbitrary",)),
        name="mem_attn_ln",
    )(xb, x, kv, wq, wo, g, b, rw, rb)


def _route_kernel(lg_ref, pos_ref, gate_ref, meta_ref, cnt_ref, run_ref, pst_ref,
                  *, n_experts, tm, tt):
    ph = pl.program_id(0)
    i = pl.program_id(1)
    lane = lax.broadcasted_iota(jnp.int32, (tt, LANES), 1)
    lanef = lane.astype(F32)
    work = jnp.where(lane < n_experts, lg_ref[...], -jnp.inf)
    vals, hots = [], []
    for _ in range(TOP_K):
        m = jnp.max(work, axis=-1, keepdims=True)
        idx = jnp.min(jnp.where(work == m, lanef, float(LANES)), axis=-1, keepdims=True)
        hot = lanef == idx
        vals.append(m)
        hots.append(hot)
        work = jnp.where(hot, -jnp.inf, work)
    sel = functools.reduce(jnp.logical_or, hots).astype(F32)
    colsum = jnp.sum(sel, axis=0, keepdims=True)

    @pl.when((ph == 0) & (i == 0))
    def _():
        cnt_ref[...] = jnp.zeros_like(cnt_ref)

    @pl.when(ph == 0)
    def _():
        cnt_ref[...] += colsum

    @pl.when((ph == 1) & (i == 0))
    def _():
        cnt = cnt_ref[...]
        nblk = jnp.floor((cnt + (tm - 1)) / tm)
        a = lax.broadcasted_iota(jnp.int32, (LANES, LANES), 0)
        b = lax.broadcasted_iota(jnp.int32, (LANES, LANES), 1)
        upper = (a < b).astype(BF16)
        first = _dot(jnp.broadcast_to(nblk, (8, LANES)).astype(BF16), upper)[0:1]
        pst_ref[...] = first * tm
        run_ref[...] = jnp.zeros_like(run_ref)
        row = lax.broadcasted_iota(jnp.int32, meta_ref.shape, 0)
        meta = jnp.where(row == 0, cnt, jnp.where(row == 1, first * tm, jnp.where(row == 2, first + nblk, 0.0)))
        meta_ref[...] = meta.astype(jnp.int32)

    @pl.when(ph == 1)
    def _():
        r = lax.broadcasted_iota(jnp.int32, (tt, tt), 0)
        c = lax.broadcasted_iota(jnp.int32, (tt, tt), 1)
        before = _dot((r > c).astype(BF16), sel.astype(BF16)) + run_ref[...]
        posfull = pst_ref[...] + before
        e = [jnp.exp(v - vals[0]) for v in vals]
        den = functools.reduce(lambda x, y: x + y, e)
        pos = jnp.zeros((tt, LANES), F32)
        gat = jnp.zeros((tt, LANES), F32)
        for k in range(TOP_K):
            pk = jnp.sum(jnp.where(hots[k], posfull, 0.0), axis=-1, keepdims=True)
            pos = jnp.where(lane == k, pk, pos)
            gat = jnp.where(lane == k, e[k] / den, gat)
        pos_ref[...] = pos.astype(jnp.int32)
        gate_ref[...] = gat
        run_ref[...] += colsum


def route(logits, n_experts, *, tm, tt):
    t = logits.shape[0]
    tile = lambda: pl.BlockSpec((tt, LANES), lambda p, i: (i * p, 0))
    kern = functools.partial(_route_kernel, n_experts=n_experts, tm=tm, tt=tt)
    return pl.pallas_call(
        kern,
        grid=(2, t // tt),
        in_specs=[pl.BlockSpec((tt, LANES), lambda p, i: (i, 0))],
        out_specs=[tile(), tile(), pl.BlockSpec((8, LANES), lambda p, i: (0, 0))],
        out_shape=[jax.ShapeDtypeStruct((t, LANES), jnp.int32), jax.ShapeDtypeStruct((t, LANES), F32),
                   jax.ShapeDtypeStruct((8, LANES), jnp.int32)],
        scratch_shapes=[pltpu.VMEM((1, LANES), F32), pltpu.VMEM((1, LANES), F32), pltpu.VMEM((1, LANES), F32)],
        compiler_params=_cparams(("arbitrary", "arbitrary")),
        name="moe_route",
    )(logits)


def _row_copy(src, si, dst, di, sem):
    return pltpu.make_async_copy(src.at[pl.ds(si, 1), :], dst.at[pl.ds(di, 1), :], sem)


def _dispatch_kernel(cnt_ref, pst_ref, nu_ref, pos_ref, x_ref, xs_ref, zero_scr, sem, *, n_experts, tm, tt):
    i = pl.program_id(0)
    p_rows = xs_ref.shape[0]

    @pl.when(i == 0)
    def _():
        zero_scr[...] = jnp.zeros_like(zero_scr)
        zrows = zero_scr.shape[0]

        def pad_copy(e):
            start = jnp.minimum((pst_ref[e] + cnt_ref[e]) // SUBLANES * SUBLANES, p_rows - zrows)
            return pltpu.make_async_copy(zero_scr, xs_ref.at[pl.ds(pl.multiple_of(start, SUBLANES), zrows), :], sem)

        def start_pad(e, c):
            pad_copy(e).start()
            return c

        def wait_pad(e, c):
            pad_copy(e).wait()
            return c

        lax.fori_loop(0, n_experts, start_pad, 0)
        lax.fori_loop(0, n_experts, wait_pad, 0)

        def zero_tail(blk, c):
            cp = pltpu.make_async_copy(zero_scr.at[pl.ds(0, tm), :],
                                       xs_ref.at[pl.ds(pl.multiple_of(blk * tm, tm), tm), :], sem)
            cp.start()
            cp.wait()
            return c

        lax.fori_loop(nu_ref[0], p_rows // tm, zero_tail, 0)

    def start_rows(r8, c):
        for rr in range(SUBLANES):
            for k in range(TOP_K):
                p = pos_ref[(r8 * SUBLANES + rr) * TOP_K + k]
                _row_copy(x_ref.at[r8], rr, xs_ref, p, sem).start(priority=k % 2)
        return c

    def wait_rows(r, c):
        for k in range(TOP_K):
            _row_copy(x_ref.at[0], 0, xs_ref, 0, sem).wait()
        return c

    lax.fori_loop(0, tt // SUBLANES, start_rows, 0)
    lax.fori_loop(0, tt, wait_rows, 0, unroll=2)


def moe_dispatch(xpk, pos_flat, cnt, pst, n_used, p_rows, *, n_experts, tm, tt):
    t, dh = xpk.shape
    kern = functools.partial(_dispatch_kernel, n_experts=n_experts, tm=tm, tt=tt)
    return pl.pallas_call(
        kern,
        grid_spec=pltpu.PrefetchScalarGridSpec(
            num_scalar_prefetch=3,
            grid=(t // tt,),
            in_specs=[pl.BlockSpec((tt * TOP_K,), lambda i, *_: (i,), memory_space=pltpu.SMEM),
                      pl.BlockSpec((tt // SUBLANES, SUBLANES, dh), lambda i, *_: (i, 0, 0))],
            out_specs=pl.BlockSpec(memory_space=pl.ANY),
            scratch_shapes=[pltpu.VMEM((tm + SUBLANES, dh), U32), pltpu.SemaphoreType.DMA(())],
        ),
        out_shape=jax.ShapeDtypeStruct((p_rows, dh), U32),
        compiler_params=_cparams(("arbitrary",)),
        name="moe_dispatch",
    )(cnt, pst, n_used, pos_flat, xpk.reshape(t // SUBLANES, SUBLANES, dh))


def _new_expert(be_ref, i):
    prev = be_ref[jnp.maximum(i - 1, 0)]
    return (i == 0) | (be_ref[i] != prev)


def _by_block_rows(rows, tm, compute, out_ref):
    half = tm // 2

    def part():
        compute(half)
        out_ref[half:, :] = jnp.zeros((tm - half, out_ref.shape[1]), out_ref.dtype)

    pl.when(rows > half)(functools.partial(compute, tm))
    pl.when((rows > 0) & (rows <= half))(part)

    @pl.when(rows == 0)
    def _():
        out_ref[...] = jnp.zeros_like(out_ref)


def _moe_gu_kernel(be_ref, br_ref, x_ref, wg_ref, wu_ref, bg_ref, bu_ref, a_ref, wgb, wub):
    i = pl.program_id(1)
    tm, hw = x_ref.shape

    @pl.when(_new_expert(be_ref, i))
    def _():
        wgb[...] = wg_ref[...].astype(BF16)
        wub[...] = wu_ref[...].astype(BF16)

    def compute(n):
        lo, hi = _unpack_halves(x_ref[:n, :])
        gate = _dot(lo, wgb[:hw, :]) + _dot(hi, wgb[hw:, :]) + bg_ref[...]
        up = _dot(lo, wub[:hw, :]) + _dot(hi, wub[hw:, :]) + bu_ref[...]
        gate = jnp.minimum(gate, SWIGLU_LIMIT)
        up = jnp.clip(up, -SWIGLU_LIMIT, SWIGLU_LIMIT)
        sig = 1.0 / (1.0 + jnp.exp(-SWIGLU_ALPHA * gate))
        a_ref[:n, :] = (gate * sig * (up + 1.0)).astype(a_ref.dtype)

    _by_block_rows(br_ref[i], tm, compute, a_ref)


def _used_block(i, br):
    return jnp.where(br[i] > 0, i, 0)


def moe_gate_up(xs, w_gu, b_gu, layer, blk_e, blk_rows, *, tm, tn):
    p, hw = xs.shape
    d = 2 * hw
    f = w_gu.shape[3] // 2
    assert f % tn == 0 and p % tm == 0
    nj = f // tn
    wspec = lambda off: pl.BlockSpec((None, None, d, tn), lambda j, i, be, nu: (layer, be[i], 0, off + j))
    bspec = lambda off: pl.BlockSpec((None, None, 1, tn), lambda j, i, be, nu: (layer, be[i], 0, off + j))
    return pl.pallas_call(
        _moe_gu_kernel,
        grid_spec=pltpu.PrefetchScalarGridSpec(
            num_scalar_prefetch=2,
            grid=(nj, p // tm),
            in_specs=[pl.BlockSpec((tm, hw), lambda j, i, be, nu: (_used_block(i, nu), 0)),
                      wspec(0), wspec(nj), bspec(0), bspec(nj)],
            out_specs=pl.BlockSpec((tm, tn), lambda j, i, be, nu: (i, j)),
            scratch_shapes=[pltpu.VMEM((d, tn), BF16), pltpu.VMEM((d, tn), BF16)],
        ),
        out_shape=jax.ShapeDtypeStruct((p, f), BF16),
        compiler_params=_cparams(("arbitrary", "arbitrary")),
        name="moe_gate_up",
    )(blk_e, blk_rows, xs, w_gu, w_gu, b_gu, b_gu)


def _moe_down_kernel(be_ref, br_ref, a_ref, w_ref, b_ref, y_ref, wb):
    i = pl.program_id(1)

    @pl.when(_new_expert(be_ref, i))
    def _():
        wb[...] = w_ref[...].astype(BF16)

    def compute(n):
        y_ref[:n, :] = _dot(a_ref[:n, :], wb[...]) + b_ref[...]

    _by_block_rows(br_ref[i], a_ref.shape[0], compute, y_ref)


def moe_down(a, w_down, b_down, layer, blk_e, blk_rows, *, tm, tn):
    p, f = a.shape
    d = w_down.shape[3]
    assert d % tn == 0 and p % tm == 0
    return pl.pallas_call(
        _moe_down_kernel,
        grid_spec=pltpu.PrefetchScalarGridSpec(
            num_scalar_prefetch=2,
            grid=(d // tn, p // tm),
            in_specs=[pl.BlockSpec((tm, f), lambda j, i, be, nu: (i, 0)),
                      pl.BlockSpec((None, None, f, tn), lambda j, i, be, nu: (layer, be[i], 0, j)),
                      pl.BlockSpec((None, None, 1, tn), lambda j, i, be, nu: (layer, be[i], 0, j))],
            out_specs=pl.BlockSpec((tm, tn), lambda j, i, be, nu: (i, j)),
            scratch_shapes=[pltpu.VMEM((f, tn), BF16)],
        ),
        out_shape=jax.ShapeDtypeStruct((p, d), F32),
        compiler_params=_cparams(("arbitrary", "arbitrary")),
        name="moe_down",
    )(blk_e, blk_rows, a, w_down, b_down)


def _combine_kernel(pos_ref, posn_ref, gate_ref, x_ref, g_ref, b_ref, ys_ref, o_ref, ob_ref, buf, sem,
                    *, alpha, tt):
    i = pl.program_id(0)
    n = pl.num_programs(0)
    slot = i % 2

    def start_tile(p_ref, s):
        def body(r8, c):
            for rr in range(SUBLANES):
                for k in range(TOP_K):
                    p = p_ref[(r8 * SUBLANES + rr) * TOP_K + k]
                    _row_copy(ys_ref, p, buf.at[s, k, r8], rr, sem.at[s]).start(priority=k % 2)
            return c
        lax.fori_loop(0, tt // SUBLANES, body, 0)

    @pl.when(i == 0)
    def _():
        start_tile(pos_ref, 0)

    @pl.when(i + 1 < n)
    def _():
        start_tile(posn_ref, 1 - slot)

    def wait_rows(r, c):
        for k in range(TOP_K):
            _row_copy(ys_ref, 0, buf.at[slot, k, 0], 0, sem.at[slot]).wait()
        return c

    lax.fori_loop(0, tt, wait_rows, 0, unroll=2)
    gates = gate_ref[...]
    d = x_ref.shape[1]
    f = gates[:, 0:1] * buf[slot, 0].reshape(tt, d)
    for k in range(1, TOP_K):
        f = f + gates[:, k:k + 1] * buf[slot, k].reshape(tt, d)
    y = _layer_norm(alpha * x_ref[...] + f, g_ref[...], b_ref[...])
    o_ref[...] = y
    ob_ref[...] = y.astype(BF16)


def moe_combine_ln(ys, pos_flat, gates, x, g, b, *, alpha, tt):
    t, d = x.shape
    nt = t // tt
    tile = pl.BlockSpec((tt, d), lambda i: (i, 0))
    row = pl.BlockSpec((1, d), lambda i: (0, 0))
    kern = functools.partial(_combine_kernel, alpha=alpha, tt=tt)
    return pl.pallas_call(
        kern,
        grid=(nt,),
        in_specs=[pl.BlockSpec((tt * TOP_K,), lambda i: (i,), memory_space=pltpu.SMEM),
                  pl.BlockSpec((tt * TOP_K,), lambda i: (jnp.minimum(i + 1, nt - 1),), memory_space=pltpu.SMEM),
                  pl.BlockSpec((tt, LANES), lambda i: (i, 0)),
                  tile, row, row,
                  pl.BlockSpec(memory_space=pl.ANY)],
        out_specs=[tile, tile],
        out_shape=[jax.ShapeDtypeStruct((t, d), F32), jax.ShapeDtypeStruct((t, d), BF16)],
        scratch_shapes=[pltpu.VMEM((2, TOP_K, tt // SUBLANES, SUBLANES, d), F32),
                        pltpu.SemaphoreType.DMA((2,))],
        compiler_params=_cparams(("arbitrary",)),
        name="moe_combine_ln",
    )(pos_flat, pos_flat, gates, x, g, b, ys)


def kernel(x_prompt, x_sample, mem_prompt, mem_sample, ev_w_in, ev_w_out, ev_sink, ev_lam_q1, ev_lam_k1, ev_lam_q2, ev_lam_k2, ev_dnorm_w, od_w_in, od_w_out, od_conv_w, mem_wq, mem_wkv, mem_wo, ln_g, ln_b, moe_router_w, moe_router_b, moe_w_gu, moe_b_gu, moe_w_down, moe_b_down):
    b1, s1, d = x_prompt.shape
    b2, s2, _ = x_sample.shape
    segs = Segs(b1, s1, b2, s2)
    t = segs.t
    depth = ln_g.shape[0]
    n_mem = mem_prompt.shape[1]
    n_experts = moe_router_w.shape[2]
    alpha = (2 * depth) ** 0.25
    n_heads = d // (2 * HEAD_DIM)
    n_kv = n_heads // 4
    c_ch = d // 2
    dils = tuple(dil for _, dil in D_PATTERNS)

    x = jnp.concatenate([x_prompt.reshape(-1, d), x_sample.reshape(-1, d)], axis=0)
    xb = x.astype(BF16)
    memb = jnp.concatenate([mem_prompt.reshape(-1, d), mem_sample.reshape(-1, d)], axis=0).astype(BF16)
    tables = rope_tables(max(s1, s2))
    dtabs = dilated_tables(tables)
    ev_modes = jnp.array([MODE_ROPE128_Q] * n_heads + [MODE_ROPE128] * n_kv + [MODE_NONE] * n_kv
                         + [MODE_ROPE64_Q] * n_heads + [MODE_ROPE64] * n_heads
                         + [MODE_NONE] * n_heads, jnp.int32)
    od_modes = jnp.zeros((3 * c_ch // LANES,), jnp.int32)
    mem_modes = jnp.zeros((mem_wkv.shape[2] // LANES,), jnp.int32)
    mem_segs = Segs(b1, n_mem, b2, n_mem)
    rw_pad = jnp.pad(moe_router_w, ((0, 0), (0, 0), (0, LANES - n_experts)))
    rb_pad = jnp.pad(moe_router_b, ((0, 0), (0, LANES - n_experts)))[:, None, :]
    b_gu4 = moe_b_gu[:, :, None, :]
    b_down4 = moe_b_down[:, :, None, :]
    moe_tm = 512
    n_blocks = -(-(t * TOP_K + n_experts * (moe_tm - 1)) // moe_tm)
    p_rows = n_blocks * moe_tm

    for l in range(depth):
        j = l // 2
        if l % 2 == 0:
            lambda_init = 0.8 - 0.6 * math.exp(-0.3 * l)
            w_f = ev_w_in[j]
            kdim = w_f.shape[0]
            nb_lo, nb_hi = (n_heads + 2 * n_kv) * LANES, (3 * n_heads + 2 * n_kv) * LANES
            w_bqk = w_f[:, nb_lo:nb_hi].reshape(kdim, 2 * n_heads, 2, 2, B_DIM // 2)
            w_bqk = w_bqk.swapaxes(2, 3).reshape(kdim, nb_hi - nb_lo)
            w_in = jnp.concatenate([w_f[:, :nb_lo], w_bqk, w_f[:, nb_hi:]], axis=1).astype(BF16)
            n_in = w_in.shape[1]
            tn = n_in // 6 if (n_in // 6) % LANES == 0 else n_in // 18
            h = proj(xb, w_in, ev_modes, tables, segs, col0=0, ncols=n_in, tm=1024, tn=tn,
                     out_dtype=BF16)
            lamv = jnp.stack([ev_lam_q1[j], ev_lam_k1[j], ev_lam_q2[j], ev_lam_k2[j]])
            oa, ob = [], []
            for row0, bsz, s in segs.groups():
                oa.append(window_attention(h, ev_sink[j], row0, bsz, s, d_model=d))
                ob.append(diff_attention(h, lamv, ev_dnorm_w[j][None, :], row0, bsz, s, lambda_init,
                                         d_model=d, tq=1024, tk=2048, n_streams=4))
            fs = [jnp.concatenate(oa, axis=0), jnp.concatenate(ob, axis=0)]
            w_out = ev_w_out[j].astype(BF16)
        else:
            w_in = od_w_in[j].astype(BF16)
            hc = proj(xb, w_in, od_modes, tables, segs, col0=0, ncols=3 * c_ch, tm=1024,
                      tn=c_ch, out_dtype=F32)
            hqs = [proj_dilated(xb, w_in, dtabs, segs, g, dil, col0=3 * c_ch, tm=1024)
                   for g, dil in enumerate(dils)]
            oc, od = [], []
            for row0, bsz, s in segs.groups():
                oc.append(gated_conv(hc, od_conv_w[j], row0, bsz, s, ts=512, tc=c_ch // 2))
                outs = [dilated_group(hq, dil, row0, bsz, s) for hq, dil in zip(hqs, dils)]
                od.append(dilated_combine([o for o, _ in outs], [ls for _, ls in outs], dils, tm=512))
            fs = [jnp.concatenate(oc, axis=0), jnp.concatenate(od, axis=0)]
            w_out = od_w_out[j].astype(BF16)
        x, xb = outproj_ln(fs, w_out, x, ln_g[l, 0][None, :], ln_b[l, 0][None, :], alpha=alpha, tm=512)

        kv = proj(memb, mem_wkv[l].astype(BF16), mem_modes, tables, mem_segs, col0=0,
                  ncols=mem_wkv.shape[2], tm=n_mem, tn=mem_wkv.shape[2] // 2, out_dtype=BF16)
        x, xpk, logits = mem_attention_ln(xb, x, kv, mem_wq[l].astype(BF16), mem_wo[l].astype(BF16),
                                          ln_g[l, 1][None, :], ln_b[l, 1][None, :], rw_pad[l], rb_pad[l],
                                          segs, alpha=alpha, tm=512, n_mem=n_mem)

        pos, gates, meta = route(logits, n_experts, tm=moe_tm, tt=512)
        pos_flat = pos[:, :TOP_K].reshape(-1)
        cnt, pst, end_blk = meta[0, :n_experts], meta[1, :n_experts], meta[2, :n_experts]
        blk_e = jnp.minimum(jnp.sum(end_blk[None, :] <= jnp.arange(n_blocks, dtype=jnp.int32)[:, None], axis=1),
                            n_experts - 1).astype(jnp.int32)
        n_used = end_blk[n_experts - 1:]
        blk_lo = (jnp.arange(n_blocks, dtype=jnp.int32) * moe_tm)[:, None]
        blk_rows = jnp.sum(jnp.clip(jnp.minimum((pst + cnt)[None, :], blk_lo + moe_tm)
                                    - jnp.maximum(pst[None, :], blk_lo), 0, moe_tm), axis=1).astype(jnp.int32)
        xs = moe_dispatch(xpk, pos_flat, cnt, pst, n_used, p_rows, n_experts=n_experts, tm=moe_tm, tt=256)
        a = moe_gate_up(xs, moe_w_gu, b_gu4, l, blk_e, blk_rows, tm=moe_tm, tn=1024)
        ys = moe_down(a, moe_w_down, b_down4, l, blk_e, blk_rows, tm=moe_tm, tn=d)
        x, xb = moe_combine_ln(ys, pos_flat, gates, x, ln_g[l, 2][None, :], ln_b[l, 2][None, :],
                               alpha=alpha, tt=256)

    y1 = x[:segs.t1].reshape(b1, s1, d)
    y2 = x[segs.t1:].reshape(b2, s2, d)
    return (y1, y2)
```
